```python
import math
import jax, jax.numpy as jnp
from jax import lax
import numpy as np

D_MODEL = 1024
BATCH = 4
SEQ = 8192
DEPTH = 2

CHUNK = 64
DN_HEADS = 4
DN_DK = 128
DN_DV = 128
DN_CONV = 4
SG_GROUPS = 4
SG_GROUP_DIM = 128
SG_BLOCK = 128
FFN_DIM = 2816
FFN_CONV = 3

LN_EPS = 1e-5
RMS_EPS = 1e-6
L2_EPS = 1e-6
DEEPNORM_ALPHA = (2 * DEPTH) ** 0.25
DEEPNORM_BETA = (8 * DEPTH) ** -0.25

QK_W = DN_HEADS * DN_DK
V_W = DN_HEADS * DN_DV
SG_W = SG_GROUPS * SG_GROUP_DIM
IN_SPLIT_SIZES = (2 * QK_W + V_W, V_W, DN_HEADS, DN_HEADS, 2 * SG_W, 2 * D_MODEL)
IN_COLS = sum(IN_SPLIT_SIZES)

kernel_name = "hybrid_deltanet_spatialgate_convffn_deepnorm"


def _split_points():
    pts, acc = [], 0
    for s in IN_SPLIT_SIZES[:-1]:
        acc += s
        pts.append(acc)
    return pts


def layer_norm(x, g, b):
    xf = x.astype(jnp.float32)
    mu = jnp.mean(xf, axis=-1, keepdims=True)
    var = jnp.mean(jnp.square(xf - mu), axis=-1, keepdims=True)
    y = (xf - mu) * lax.rsqrt(var + LN_EPS) * g.astype(jnp.float32) + b.astype(jnp.float32)
    return y.astype(x.dtype)


def causal_depthwise_conv(x, w):
    K, C = w.shape
    return lax.conv_general_dilated(
        x, w[:, None, :].astype(x.dtype), window_strides=(1,), padding=[(K - 1, 0)],
        dimension_numbers=("NWC", "WIO", "NWC"), feature_group_count=C)


def l2norm(x):
    return x * lax.rsqrt(jnp.sum(x * x, axis=-1, keepdims=True) + L2_EPS)


def _to_chunks(t, n):
    b = t.shape[0]
    t = t.reshape((b, n, CHUNK) + t.shape[2:])
    return jnp.swapaxes(t, 2, 3)


def gated_delta_rule(q, k, v, beta, g):
    B, S, H, Dk = q.shape
    Dv = v.shape[-1]
    n = S // CHUNK
    q, k, v = _to_chunks(q, n), _to_chunks(k, n), _to_chunks(v, n)
    beta, g = _to_chunks(beta, n), _to_chunks(g, n)
    G = jnp.cumsum(g, axis=-1)
    causal = jnp.tril(jnp.ones((CHUNK, CHUNK), dtype=bool))
    strict = jnp.tril(jnp.ones((CHUNK, CHUNK), dtype=bool), k=-1)
    diff = G[..., :, None] - G[..., None, :]
    decay = jnp.exp(jnp.where(causal, diff, -jnp.inf))
    kb = k * beta[..., None]
    L = jnp.where(strict, jnp.einsum("bnhid,bnhjd->bnhij", kb, k) * decay, 0.0)
    eye = jnp.eye(CHUNK, dtype=jnp.float32)
    T = lax.linalg.triangular_solve(eye + L, jnp.broadcast_to(eye, L.shape),
                                    left_side=True, lower=True)
    W = jnp.einsum("bnhij,bnhjd->bnhid", T, kb * jnp.exp(G)[..., None])
    U = jnp.einsum("bnhij,bnhjd->bnhid", T, v * beta[..., None])
    A_qk = jnp.einsum("bnhid,bnhjd->bnhij", q, k) * decay
    q_g = q * jnp.exp(G)[..., None]
    G_last = G[..., -1]
    k_d = k * jnp.exp(G_last[..., None] - G)[..., None]
    g_last = jnp.exp(G_last)

    def step(state, inp):
        qg, kd, w, u, aqk, gl = inp
        u_new = u - jnp.einsum("bhck,bhkv->bhcv", w, state)
        o = jnp.einsum("bhck,bhkv->bhcv", qg, state) + jnp.einsum("bhij,bhjv->bhiv", aqk, u_new)
        state = state * gl[..., None, None] + jnp.einsum("bhck,bhcv->bhkv", kd, u_new)
        return state, o

    xs = tuple(jnp.moveaxis(t, 1, 0) for t in (q_g, k_d, W, U, A_qk, g_last))
    s0 = jnp.zeros((B, H, Dk, Dv), jnp.float32)
    _, o = lax.scan(step, s0, xs)
    o = jnp.swapaxes(jnp.moveaxis(o, 0, 1), 2, 3)
    return o.reshape(B, S, H, Dv)


def deltanet_branch(qkv, z, beta_logit, a, conv_w, a_log, dt_bias, norm_w):
    B, S, _ = qkv.shape
    qkv = jax.nn.silu(causal_depthwise_conv(qkv, conv_w))
    q, k, v = jnp.split(qkv, [QK_W, 2 * QK_W], axis=-1)
    f32 = jnp.float32
    q = l2norm(q.reshape(B, S, DN_HEADS, DN_DK).astype(f32)) * (DN_DK ** -0.5)
    k = l2norm(k.reshape(B, S, DN_HEADS, DN_DK).astype(f32))
    v = v.reshape(B, S, DN_HEADS, DN_DV).astype(f32)
    beta = jax.nn.sigmoid(beta_logit.astype(f32))
    g = -jnp.exp(a_log.astype(f32)) * jax.nn.softplus(a.astype(f32) + dt_bias.astype(f32))
    o = gated_delta_rule(q, k, v, beta, g)
    o = o * lax.rsqrt(jnp.mean(o * o, axis=-1, keepdims=True) + RMS_EPS) * norm_w.astype(f32)
    o = o * jax.nn.silu(z.reshape(B, S, DN_HEADS, DN_DV).astype(f32))
    return o.reshape(B, S, V_W).astype(qkv.dtype)


def spatial_gating_branch(uv, ln_g, ln_b, w_s, b_s):
    B, S, _ = uv.shape
    u, v = jnp.split(jax.nn.gelu(uv), 2, axis=-1)
    v = layer_norm(v, ln_g, ln_b)
    n = S // SG_BLOCK
    v = v.reshape(B, n, SG_BLOCK, SG_GROUPS, SG_GROUP_DIM)
    mask = jnp.tril(jnp.ones((SG_BLOCK, SG_BLOCK), dtype=bool))
    w = jnp.where(mask, w_s, 0.0).astype(v.dtype)
    mixed = jnp.einsum("gpq,bnqgc->bnpgc", w, v) + b_s.T[:, :, None].astype(v.dtype)
    return u * mixed.reshape(B, S, SG_W)


def setup_inputs(seed: int = 0) -> dict:
    key = jax.random.key(seed)
    ks = jax.random.split(key, 24)
    nrm = jax.random.normal
    D = D_MODEL
    x = nrm(ks[0], (BATCH, SEQ, D), jnp.float32)
    w_in = nrm(ks[1], (DEPTH, D, IN_COLS), jnp.float32) * D ** -0.5
    conv_qkv = nrm(ks[2], (DEPTH, DN_CONV, 2 * QK_W + V_W), jnp.float32) * DN_CONV ** -0.5
    a_log = jnp.log(jax.random.uniform(ks[3], (DEPTH, DN_HEADS), jnp.float32, 1.0, 16.0))
    dt = jnp.exp(jax.random.uniform(ks[4], (DEPTH, DN_HEADS), jnp.float32,
                                    math.log(1e-3), math.log(1e-1)))
    dt_bias = dt + jnp.log(-jnp.expm1(-dt))
    dn_norm_w = 1.0 + 0.02 * nrm(ks[5], (DEPTH, DN_DV), jnp.float32)
    w_branch_a = nrm(ks[6], (DEPTH, V_W, D), jnp.float32) * V_W ** -0.5 * DEEPNORM_BETA
    sg_ln_g = 1.0 + 0.02 * nrm(ks[7], (DEPTH, SG_W), jnp.float32)
    sg_ln_b = 0.02 * nrm(ks[8], (DEPTH, SG_W), jnp.float32)
    w_spatial = nrm(ks[9], (DEPTH, SG_GROUPS, SG_BLOCK, SG_BLOCK), jnp.float32) * SG_BLOCK ** -0.5
    b_spatial = 1.0 + 0.02 * nrm(ks[10], (DEPTH, SG_GROUPS, SG_BLOCK), jnp.float32)
    w_branch_b = nrm(ks[11], (DEPTH, SG_W, D), jnp.float32) * SG_W ** -0.5 * DEEPNORM_BETA
    w_out = nrm(ks[12], (DEPTH, D, D), jnp.float32) * D ** -0.5 * DEEPNORM_BETA
    ln1_g = 1.0 + 0.02 * nrm(ks[13], (DEPTH, D), jnp.float32)
    ln1_b = 0.02 * nrm(ks[14], (DEPTH, D), jnp.float32)
    w_up = nrm(ks[15], (DEPTH, D, 2 * FFN_DIM), jnp.float32) * D ** -0.5
    conv_ffn = nrm(ks[16], (DEPTH, FFN_CONV, 2 * FFN_DIM), jnp.float32) * FFN_CONV ** -0.5
    w_down = nrm(ks[17], (DEPTH, FFN_DIM, D), jnp.float32) * FFN_DIM ** -0.5 * DEEPNORM_BETA
    ln2_g = 1.0 + 0.02 * nrm(ks[18], (DEPTH, D), jnp.float32)
    ln2_b = 0.02 * nrm(ks[19], (DEPTH, D), jnp.float32)
    return {"x": x, "w_in": w_in, "conv_qkv": conv_qkv, "a_log": a_log, "dt_bias": dt_bias,
            "dn_norm_w": dn_norm_w, "w_branch_a": w_branch_a, "sg_ln_g": sg_ln_g,
            "sg_ln_b": sg_ln_b, "w_spatial": w_spatial, "b_spatial": b_spatial,
            "w_branch_b": w_branch_b, "w_out": w_out, "ln1_g": ln1_g, "ln1_b": ln1_b,
            "w_up": w_up, "conv_ffn": conv_ffn, "w_down": w_down, "ln2_g": ln2_g, "ln2_b": ln2_b}


def reference(x, w_in, conv_qkv, a_log, dt_bias, dn_norm_w, w_branch_a, sg_ln_g, sg_ln_b,
              w_spatial, b_spatial, w_branch_b, w_out, ln1_g, ln1_b, w_up, conv_ffn, w_down,
              ln2_g, ln2_b):
    pts = _split_points()
    for l in range(DEPTH):
        proj = x @ w_in[l]
        qkv, z, beta_logit, a, sg_uv, gates = jnp.split(proj, pts, axis=-1)
        o_a = deltanet_branch(qkv, z, beta_logit, a, conv_qkv[l], a_log[l], dt_bias[l], dn_norm_w[l])
        o_b = spatial_gating_branch(sg_uv, sg_ln_g[l], sg_ln_b[l], w_spatial[l], b_spatial[l])
        gate_a, gate_b = jnp.split(jax.nn.sigmoid(gates), 2, axis=-1)
        h = gate_a * (o_a @ w_branch_a[l]) + gate_b * (o_b @ w_branch_b[l])
        x = layer_norm(DEEPNORM_ALPHA * x + h @ w_out[l], ln1_g[l], ln1_b[l])
        up = causal_depthwise_conv(x @ w_up[l], conv_ffn[l])
        a_ff, b_ff = jnp.split(up, 2, axis=-1)
        x = layer_norm(DEEPNORM_ALPHA * x + (jax.nn.silu(a_ff) * b_ff) @ w_down[l], ln2_g[l], ln2_b[l])
    return x
```

```python
import functools

import jax
import jax.numpy as jnp
from jax import lax
from jax.experimental import pallas as pl
from jax.experimental.pallas import tpu as pltpu

F32 = jnp.float32
BF16 = jnp.bfloat16

D_MODEL = 1024
DEPTH = 2
CHUNK = 64
DN_HEADS = 4
DN_DK = 128
DN_DV = 128
DN_CONV = 4
SG_GROUPS = 4
SG_GROUP_DIM = 128
SG_BLOCK = 128
FFN_DIM = 2816
FFN_CONV = 3
LN_EPS = 1e-5
RMS_EPS = 1e-6
L2_EPS = 1e-6
DEEPNORM_ALPHA = (2 * DEPTH) ** 0.25

QK_W = DN_HEADS * DN_DK
V_W = DN_HEADS * DN_DV
SG_W = SG_GROUPS * SG_GROUP_DIM
QKV_W = 2 * QK_W + V_W
QKVZ_W = QKV_W + V_W
BD_OFF = QKVZ_W
UV_OFF = BD_OFF + 2 * DN_HEADS
GATE_OFF = UV_OFF + 2 * SG_W

LANES = 128
SUBLANES = 8
PAIR = 2 * CHUNK
FFN_CHUNK = 256
N_FFN_CHUNKS = FFN_DIM // FFN_CHUNK
TOKEN_TILE = 512
VMEM_LIMIT = 56 * 1024 * 1024


def _layer_norm(r, g, b):
    mu = jnp.mean(r, axis=-1, keepdims=True)
    d = r - mu
    var = jnp.mean(d * d, axis=-1, keepdims=True)
    return d * lax.rsqrt(var + LN_EPS) * g + b


def _dot(a, b):
    return jnp.dot(a, b, preferred_element_type=F32)


def _dot_nt(a, b):
    return lax.dot_general(a, b, (((1,), (1,)), ((), ())), preferred_element_type=F32)


def _dot_tn(a, b):
    return lax.dot_general(a, b, (((0,), (0,)), ((), ())), preferred_element_type=F32)


def _in_proj_kernel(x_ref, wq_ref, wbd_ref, wuv_ref, cw_ref, alog_ref, dtb_ref, lng_ref, lnb_ref,
                    ws_ref, bst_ref, q_ref, k_ref, v_ref, sz_ref, bg_ref, ob_ref, cbuf_ref, *, tm):
    halo = SUBLANES

    @pl.when(pl.program_id(1) == 0)
    def _():
        cbuf_ref[0:halo, :] = jnp.zeros((halo, QKV_W), F32)

    xb = x_ref[...].astype(BF16)
    proj = _dot(xb, wq_ref[...])
    pre = proj[:, :QKV_W]
    cbuf_ref[halo:halo + tm, :] = pre
    acc = pre * cw_ref[DN_CONV - 1:DN_CONV, :]
    for j in range(DN_CONV - 1):
        sh = DN_CONV - 1 - j
        acc = acc + cbuf_ref[halo - sh:halo - sh + tm, :] * cw_ref[j:j + 1, :]
    cbuf_ref[0:halo, :] = cbuf_ref[tm:tm + halo, :]
    qkv = acc * jax.nn.sigmoid(acc)
    for h in range(DN_HEADS):
        sl = slice(h * DN_DK, (h + 1) * DN_DK)
        qh = qkv[:, sl]
        qn = qh * lax.rsqrt(jnp.sum(qh * qh, axis=-1, keepdims=True) + L2_EPS) * (DN_DK ** -0.5)
        q_ref[:, sl] = qn.astype(BF16)
        kh = qkv[:, QK_W + h * DN_DK:QK_W + (h + 1) * DN_DK]
        kn = kh * lax.rsqrt(jnp.sum(kh * kh, axis=-1, keepdims=True) + L2_EPS)
        k_ref[:, sl] = kn.astype(BF16)
    v_ref[...] = qkv[:, 2 * QK_W:].astype(BF16)
    z = proj[:, QKV_W:]
    sz_ref[...] = (z * jax.nn.sigmoid(z)).astype(BF16)

    bd = _dot(xb, wbd_ref[...])
    beta = jax.nn.sigmoid(bd)
    a = bd + dtb_ref[...]
    g = -jnp.exp(alog_ref[...]) * (jnp.maximum(a, 0.0) + jnp.log1p(jnp.exp(-jnp.abs(a))))
    lane = lax.broadcasted_iota(jnp.int32, bd.shape, 1)
    bg_ref[...] = jnp.where(lane < DN_HEADS, beta, g)[:, :2 * DN_HEADS]

    uv = _dot(xb, wuv_ref[...])
    ge = uv * (0.5 * (1.0 + jnp.tanh(0.7978845608028654 * (uv + 0.044715 * (uv * uv * uv)))))
    u = ge[:, :SG_W]
    vln = _layer_norm(ge[:, SG_W:], lng_ref[...], lnb_ref[...]).astype(BF16)
    ri = lax.broadcasted_iota(jnp.int32, (SG_BLOCK, SG_BLOCK), 0)
    ci = lax.broadcasted_iota(jnp.int32, (SG_BLOCK, SG_BLOCK), 1)
    nb = tm // SG_BLOCK
    for gi in range(SG_GROUPS):
        cs = slice(gi * SG_GROUP_DIM, (gi + 1) * SG_GROUP_DIM)
        wm = jnp.where(ci <= ri, ws_ref[gi], 0.0).astype(BF16)
        rhs = jnp.concatenate([vln[r * SG_BLOCK:(r + 1) * SG_BLOCK, cs] for r in range(nb)], axis=1)
        m = _dot(wm, rhs) + bst_ref[:, gi:gi + 1]
        for r in range(nb):
            rs = slice(r * SG_BLOCK, (r + 1) * SG_BLOCK)
            ob_ref[rs, cs] = (u[rs, cs] * m[:, r * SG_GROUP_DIM:(r + 1) * SG_GROUP_DIM]).astype(BF16)


def _delta_kernel(q_ref, k_ref, v_ref, sz_ref, bg_ref, nw_ref, o_ref, s_ref, *, tt):
    @pl.when(pl.program_id(1) == 0)
    def _():
        s_ref[...] = jnp.zeros(s_ref.shape, F32)

    P = PAIR
    ri = lax.broadcasted_iota(jnp.int32, (P, P), 0)
    ci = lax.broadcasted_iota(jnp.int32, (P, P), 1)
    same = (ri // CHUNK) == (ci // CHUNK)
    eye = ri == ci
    causal = same & (ci <= ri)
    strict = same & (ci < ri)
    lastm = ci == (ri // CHUNK) * CHUNK + (CHUNK - 1)
    lvl = []
    s = 1
    while s < CHUNK:
        lvl.append(((ri // (2 * s)) == (ci // (2 * s))) & (((ri // s) % 2) == 1) & (((ci // s) % 2) == 0))
        s *= 2
    eyef = eye.astype(F32)
    nw = nw_ref[...]

    def pair(p, carry):
        r0 = pl.multiple_of(p * P, P)
        rows = pl.ds(r0, P)
        bgt = bg_ref[rows, :]
        for h in range(DN_HEADS):
            sl = slice(h * DN_DK, (h + 1) * DN_DK)
            q16 = q_ref[rows, sl]
            k16 = k_ref[rows, sl]
            qf = q16.astype(F32)
            kf = k16.astype(F32)
            vf = v_ref[rows, sl].astype(F32)
            bcol = bgt[:, h:h + 1]
            gcol = bgt[:, DN_HEADS + h:DN_HEADS + h + 1]
            grow = jnp.sum(jnp.where(eye, gcol, 0.0), axis=0, keepdims=True)
            Gcol = jnp.sum(jnp.where(causal, grow, 0.0), axis=1, keepdims=True)
            Grow = jnp.sum(jnp.where(eye, Gcol, 0.0), axis=0, keepdims=True)
            Glast = jnp.sum(jnp.where(lastm, Grow, 0.0), axis=1, keepdims=True)
            D = jnp.where(causal, jnp.exp(jnp.where(causal, Gcol - Grow, 0.0)), 0.0)
            eG = jnp.exp(Gcol)
            kb = kf * bcol
            kbg = kb * eG
            qg = qf * eG
            kd = kf * jnp.exp(Glast - Gcol)
            vb = vf * bcol
            L = jnp.where(strict, _dot_nt(kb.astype(BF16), k16) * D, 0.0)
            A = jnp.where(causal, _dot_nt(q16, k16) * D, 0.0)
            N = -jnp.where(lvl[0], L, 0.0)
            for m in lvl[1:]:
                C = jnp.where(m, L, 0.0)
                Nb = N.astype(BF16)
                X = C + _dot(C.astype(BF16), Nb)
                N = N - (X + _dot(Nb, X.astype(BF16)))
            T = (eyef + N).astype(BF16)
            WU = _dot(T, jnp.concatenate([kbg, vb], axis=1).astype(BF16))
            W = WU[:, :DN_DK]
            U = WU[:, DN_DK:]
            S = s_ref[h]
            outs = []
            for c in range(P // CHUNK):
                rs = slice(c * CHUNK, (c + 1) * CHUNK)
                lhs = jnp.concatenate([W[rs], qg[rs]], axis=0).astype(BF16)
                WS = _dot(lhs, S.astype(BF16))
                ub = (U[rs] - WS[:CHUNK]).astype(BF16)
                outs.append(WS[CHUNK:] + _dot(A[rs, rs].astype(BF16), ub))
                gl = jnp.exp(Glast[c * CHUNK:c * CHUNK + 1, :])
                S = S * gl + _dot_tn(kd[rs].astype(BF16), ub)
            s_ref[h] = S
            o = jnp.concatenate(outs, axis=0)
            o = o * lax.rsqrt(jnp.mean(o * o, axis=-1, keepdims=True) + RMS_EPS) * nw
            o_ref[rows, sl] = (o * sz_ref[rows, sl].astype(F32)).astype(BF16)
        return carry

    lax.fori_loop(0, tt // P, pair, 0)


def _mix_out_kernel(x_ref, oa_ref, ob_ref, wg_ref, wa_ref, wb_ref, wo_ref, g_ref, b_ref, out_ref):
    x = x_ref[...]
    gates = jax.nn.sigmoid(_dot(x.astype(BF16), wg_ref[...]))
    ya = _dot(oa_ref[...], wa_ref[...])
    yb = _dot(ob_ref[...], wb_ref[...])
    h = gates[:, :D_MODEL] * ya + gates[:, D_MODEL:] * yb
    y = _dot(h.astype(BF16), wo_ref[...])
    out_ref[...] = _layer_norm(DEEPNORM_ALPHA * x + y, g_ref[...], b_ref[...])


def _ffn_kernel(x_ref, wup_ref, cw_ref, wdn_ref, g_ref, b_ref, out_ref, carry_ref, buf_ref, acc_ref, *, tm):
    halo = SUBLANES

    @pl.when(pl.program_id(1) == 0)
    def _():
        carry_ref[...] = jnp.zeros(carry_ref.shape, F32)

    x = x_ref[...]
    xb = x.astype(BF16)
    acc_ref[...] = jnp.zeros(acc_ref.shape, F32)

    def chunk(j, carry):
        p = _dot(xb, wup_ref[j])
        buf_ref[0:halo, :] = carry_ref[j]
        buf_ref[halo:halo + tm, :] = p
        cw = cw_ref[j]
        c = p * cw[FFN_CONV - 1:FFN_CONV, :]
        for t in range(FFN_CONV - 1):
            sh = FFN_CONV - 1 - t
            c = c + buf_ref[halo - sh:halo - sh + tm, :] * cw[t:t + 1, :]
        carry_ref[j] = p[tm - halo:, :]
        a = c[:, :FFN_CHUNK]
        hm = (a * jax.nn.sigmoid(a) * c[:, FFN_CHUNK:]).astype(BF16)
        acc_ref[...] += _dot(hm, wdn_ref[j])
        return carry

    lax.fori_loop(0, N_FFN_CHUNKS, chunk, 0)
    out_ref[...] = _layer_norm(DEEPNORM_ALPHA * x + acc_ref[...], g_ref[...], b_ref[...])


def _const_spec(shape):
    nd = len(shape)
    return pl.BlockSpec(shape, lambda *_: (0,) * nd)


def _params(sem):
    return pltpu.CompilerParams(dimension_semantics=sem, vmem_limit_bytes=VMEM_LIMIT)


def _layer(xf, B, S, p):
    T = B * S
    tm = min(TOKEN_TILE, S)
    ns = S // tm
    seq = ("arbitrary", "arbitrary")

    def tok(width):
        return pl.BlockSpec((tm, width), lambda b, s: (b * ns + s, 0))

    q, k, v, sz, bg, ob = pl.pallas_call(
        functools.partial(_in_proj_kernel, tm=tm),
        grid=(B, ns),
        in_specs=[tok(D_MODEL), _const_spec(p["wq"].shape), _const_spec(p["wbd"].shape),
                  _const_spec(p["wuv"].shape), _const_spec(p["cw"].shape), _const_spec(p["alog"].shape),
                  _const_spec(p["dtb"].shape), _const_spec(p["lng"].shape), _const_spec(p["lnb"].shape),
                  _const_spec(p["ws"].shape), _const_spec(p["bst"].shape)],
        out_specs=[tok(QK_W), tok(QK_W), tok(V_W), tok(V_W), tok(2 * DN_HEADS), tok(SG_W)],
        out_shape=[jax.ShapeDtypeStruct((T, QK_W), BF16), jax.ShapeDtypeStruct((T, QK_W), BF16),
                   jax.ShapeDtypeStruct((T, V_W), BF16), jax.ShapeDtypeStruct((T, V_W), BF16),
                   jax.ShapeDtypeStruct((T, 2 * DN_HEADS), F32), jax.ShapeDtypeStruct((T, SG_W), BF16)],
        scratch_shapes=[pltpu.VMEM((tm + SUBLANES, QKV_W), F32)],
        compiler_params=_params(seq),
        name="in_proj",
    )(xf, p["wq"], p["wbd"], p["wuv"], p["cw"], p["alog"], p["dtb"], p["lng"], p["lnb"], p["ws"], p["bst"])

    oa = pl.pallas_call(
        functools.partial(_delta_kernel, tt=tm),
        grid=(B, ns),
        in_specs=[tok(QK_W), tok(QK_W), tok(V_W), tok(V_W), tok(2 * DN_HEADS), _const_spec(p["nw"].shape)],
        out_specs=tok(V_W),
        out_shape=jax.ShapeDtypeStruct((T, V_W), BF16),
        scratch_shapes=[pltpu.VMEM((DN_HEADS, DN_DK, DN_DV), F32)],
        compiler_params=_params(seq),
        name="delta",
    )(q, k, v, sz, bg, p["nw"])

    def tok1(width):
        return pl.BlockSpec((tm, width), lambda i: (i, 0))

    x1 = pl.pallas_call(
        _mix_out_kernel,
        grid=(T // tm,),
        in_specs=[tok1(D_MODEL), tok1(V_W), tok1(SG_W), _const_spec(p["wg"].shape), _const_spec(p["wa"].shape),
                  _const_spec(p["wb"].shape), _const_spec(p["wo"].shape), _const_spec(p["ln1g"].shape),
                  _const_spec(p["ln1b"].shape)],
        out_specs=tok1(D_MODEL),
        out_shape=jax.ShapeDtypeStruct((T, D_MODEL), F32),
        compiler_params=_params(("arbitrary",)),
        name="mix_out",
    )(xf, oa, ob, p["wg"], p["wa"], p["wb"], p["wo"], p["ln1g"], p["ln1b"])

    x2 = pl.pallas_call(
        functools.partial(_ffn_kernel, tm=tm),
        grid=(B, ns),
        in_specs=[tok(D_MODEL), _const_spec(p["wup"].shape), _const_spec(p["cwf"].shape),
                  _const_spec(p["wdn"].shape), _const_spec(p["ln2g"].shape), _const_spec(p["ln2b"].shape)],
        out_specs=tok(D_MODEL),
        out_shape=jax.ShapeDtypeStruct((T, D_MODEL), F32),
        scratch_shapes=[pltpu.VMEM((N_FFN_CHUNKS, SUBLANES, 2 * FFN_CHUNK), F32),
                        pltpu.VMEM((tm + SUBLANES, 2 * FFN_CHUNK), F32),
                        pltpu.VMEM((tm, D_MODEL), F32)],
        compiler_params=_params(seq),
        name="ffn",
    )(x1, p["wup"], p["cwf"], p["wdn"], p["ln2g"], p["ln2b"])
    return x2


def _pad_rows(a, rows):
    return jnp.pad(a, ((0, rows - a.shape[0]), (0, 0)))


def _lane_row(vals, offset):
    return jnp.zeros((1, LANES), F32).at[0, offset:offset + vals.shape[0]].set(vals)


def _layer_params(l, w_in, conv_qkv, a_log, dt_bias, dn_norm_w, w_branch_a, sg_ln_g, sg_ln_b, w_spatial,
                  b_spatial, w_branch_b, w_out, ln1_g, ln1_b, w_up, conv_ffn, w_down, ln2_g, ln2_b):
    wi = w_in[l]
    nc = N_FFN_CHUNKS

    def pair_chunks(a):
        lead = a.shape[:-1]
        a = a.reshape(lead + (2, nc, FFN_CHUNK))
        a = jnp.moveaxis(a, -2, 0)
        return a.reshape((nc,) + lead + (2 * FFN_CHUNK,))

    return dict(
        wq=wi[:, :QKVZ_W].astype(BF16),
        wbd=jnp.pad(wi[:, BD_OFF:UV_OFF], ((0, 0), (0, LANES - 2 * DN_HEADS))).astype(BF16),
        wuv=wi[:, UV_OFF:GATE_OFF].astype(BF16),
        wg=wi[:, GATE_OFF:].astype(BF16),
        cw=_pad_rows(conv_qkv[l], SUBLANES),
        alog=_lane_row(a_log[l], DN_HEADS),
        dtb=_lane_row(dt_bias[l], DN_HEADS),
        nw=dn_norm_w[l].reshape(1, DN_DV),
        lng=sg_ln_g[l].reshape(1, SG_W),
        lnb=sg_ln_b[l].reshape(1, SG_W),
        ws=w_spatial[l],
        bst=b_spatial[l].T,
        wa=w_branch_a[l].astype(BF16),
        wb=w_branch_b[l].astype(BF16),
        wo=w_out[l].astype(BF16),
        ln1g=ln1_g[l].reshape(1, D_MODEL),
        ln1b=ln1_b[l].reshape(1, D_MODEL),
        wup=pair_chunks(w_up[l]).astype(BF16),
        cwf=jnp.pad(pair_chunks(conv_ffn[l]), ((0, 0), (0, SUBLANES - FFN_CONV), (0, 0))),
        wdn=w_down[l].reshape(nc, FFN_CHUNK, D_MODEL).astype(BF16),
        ln2g=ln2_g[l].reshape(1, D_MODEL),
        ln2b=ln2_b[l].reshape(1, D_MODEL),
    )


def kernel(x, w_in, conv_qkv, a_log, dt_bias, dn_norm_w, w_branch_a, sg_ln_g, sg_ln_b, w_spatial, b_spatial, w_branch_b, w_out, ln1_g, ln1_b, w_up, conv_ffn, w_down, ln2_g, ln2_b):
    B, S, D = x.shape
    assert D == D_MODEL and S % min(TOKEN_TILE, S) == 0 and S % PAIR == 0
    xf = x.reshape(B * S, D)
    for l in range(w_in.shape[0]):
        p = _layer_params(l, w_in, conv_qkv, a_log, dt_bias, dn_norm_w, w_branch_a, sg_ln_g, sg_ln_b,
                          w_spatial, b_spatial, w_branch_b, w_out, ln1_g, ln1_b, w_up, conv_ffn, w_down,
                          ln2_g, ln2_b)
        xf = _layer(xf, B, S, p)
    return xf.reshape(B, S, D)
```

```python
import functools

import jax
import jax.numpy as jnp
from jax import lax
from jax.experimental import pallas as pl
from jax.experimental.pallas import tpu as pltpu

F32 = jnp.float32
BF16 = jnp.bfloat16

D_MODEL = 1024
DEPTH = 2
CHUNK = 64
DN_HEADS = 4
DN_DK = 128
DN_DV = 128
DN_CONV = 4
SG_GROUPS = 4
SG_GROUP_DIM = 128
SG_BLOCK = 128
FFN_DIM = 2816
FFN_CONV = 3
LN_EPS = 1e-5
RMS_EPS = 1e-6
L2_EPS = 1e-6
DEEPNORM_ALPHA = (2 * DEPTH) ** 0.25

QK_W = DN_HEADS * DN_DK
V_W = DN_HEADS * DN_DV
SG_W = SG_GROUPS * SG_GROUP_DIM
QKV_W = 2 * QK_W + V_W
QKVZ_W = QKV_W + V_W
BD_OFF = QKVZ_W
UV_OFF = BD_OFF + 2 * DN_HEADS
GATE_OFF = UV_OFF + 2 * SG_W

LANES = 128
SUBLANES = 8
PAIR = 2 * CHUNK
FFN_CHUNK = 256
N_FFN_CHUNKS = FFN_DIM // FFN_CHUNK
TOKEN_TILE = 512
VMEM_LIMIT = 56 * 1024 * 1024


def _layer_norm(r, g, b):
    mu = jnp.mean(r, axis=-1, keepdims=True)
    d = r - mu
    var = jnp.mean(d * d, axis=-1, keepdims=True)
    return d * lax.rsqrt(var + LN_EPS) * g + b


def _dot(a, b):
    return jnp.dot(a, b, preferred_element_type=F32)


def _dot_nt(a, b):
    return lax.dot_general(a, b, (((1,), (1,)), ((), ())), preferred_element_type=F32)


def _dot_tn(a, b):
    return lax.dot_general(a, b, (((0,), (0,)), ((), ())), preferred_element_type=F32)


def _in_proj_kernel(x_ref, wq_ref, wbd_ref, wuv_ref, cw_ref, alog_ref, dtb_ref, lng_ref, lnb_ref,
                    ws_ref, bst_ref, q_ref, k_ref, v_ref, sz_ref, bg_ref, ob_ref, cbuf_ref, *, tm):
    halo = SUBLANES

    @pl.when(pl.program_id(1) == 0)
    def _():
        cbuf_ref[0:halo, :] = jnp.zeros((halo, QKV_W), F32)

    xb = x_ref[...].astype(BF16)
    proj = _dot(xb, wq_ref[...])
    pre = proj[:, :QKV_W]
    cbuf_ref[halo:halo + tm, :] = pre
    acc = pre * cw_ref[DN_CONV - 1:DN_CONV, :]
    for j in range(DN_CONV - 1):
        sh = DN_CONV - 1 - j
        acc = acc + cbuf_ref[halo - sh:halo - sh + tm, :] * cw_ref[j:j + 1, :]
    cbuf_ref[0:halo, :] = cbuf_ref[tm:tm + halo, :]
    qkv = acc * jax.nn.sigmoid(acc)
    for h in range(DN_HEADS):
        sl = slice(h * DN_DK, (h + 1) * DN_DK)
        qh = qkv[:, sl]
        qn = qh * lax.rsqrt(jnp.sum(qh * qh, axis=-1, keepdims=True) + L2_EPS) * (DN_DK ** -0.5)
        q_ref[:, sl] = qn.astype(BF16)
        kh = qkv[:, QK_W + h * DN_DK:QK_W + (h + 1) * DN_DK]
        kn = kh * lax.rsqrt(jnp.sum(kh * kh, axis=-1, keepdims=True) + L2_EPS)
        k_ref[:, sl] = kn.astype(BF16)
    v_ref[...] = qkv[:, 2 * QK_W:].astype(BF16)
    z = proj[:, QKV_W:]
    sz_ref[...] = (z * jax.nn.sigmoid(z)).astype(BF16)

    bd = _dot(xb, wbd_ref[...])
    beta = jax.nn.sigmoid(bd)
    a = bd + dtb_ref[...]
    g = -jnp.exp(alog_ref[...]) * (jnp.maximum(a, 0.0) + jnp.log1p(jnp.exp(-jnp.abs(a))))
    lane = lax.broadcasted_iota(jnp.int32, bd.shape, 1)
    bg_ref[...] = jnp.where(lane < DN_HEADS, beta, g)[:, :2 * DN_HEADS]

    uv = _dot(xb, wuv_ref[...])
    ge = uv * (0.5 * (1.0 + jnp.tanh(0.7978845608028654 * (uv + 0.044715 * (uv * uv * uv)))))
    u = ge[:, :SG_W]
    vln = _layer_norm(ge[:, SG_W:], lng_ref[...], lnb_ref[...]).astype(BF16)
    ri = lax.broadcasted_iota(jnp.int32, (SG_BLOCK, SG_BLOCK), 0)
    ci = lax.broadcasted_iota(jnp.int32, (SG_BLOCK, SG_BLOCK), 1)
    nb = tm // SG_BLOCK
    for gi in range(SG_GROUPS):
        cs = slice(gi * SG_GROUP_DIM, (gi + 1) * SG_GROUP_DIM)
        wm = jnp.where(ci <= ri, ws_ref[gi], 0.0).astype(BF16)
        rhs = jnp.concatenate([vln[r * SG_BLOCK:(r + 1) * SG_BLOCK, cs] for r in range(nb)], axis=1)
        m = _dot(wm, rhs) + bst_ref[:, gi:gi + 1]
        for r in range(nb):
            rs = slice(r * SG_BLOCK, (r + 1) * SG_BLOCK)
            ob_ref[rs, cs] = (u[rs, cs] * m[:, r * SG_GROUP_DIM:(r + 1) * SG_GROUP_DIM]).astype(BF16)


def _bmm(a, b):
    return lax.dot_general(a, b, (((2,), (1,)), ((0,), (0,))), preferred_element_type=F32)


def _bmm_nt(a, b):
    return lax.dot_general(a, b, (((2,), (2,)), ((0,), (0,))), preferred_element_type=F32)


def _delta_prep_kernel(q_ref, k_ref, v_ref, bg_ref, w_ref, qg_ref, u_ref, a_ref, kdt_ref, gl_ref, *, tt):
    P, H = PAIR, DN_HEADS
    npair = tt // P
    ri = lax.broadcasted_iota(jnp.int32, (P, P), 0)
    ci = lax.broadcasted_iota(jnp.int32, (P, P), 1)
    same = (ri // CHUNK) == (ci // CHUNK)
    eye = ri == ci
    causal = same & (ci <= ri)
    strict = same & (ci < ri)
    lastm = ci == (ri // CHUNK) * CHUNK + (CHUNK - 1)
    lvl = []
    s = 1
    while s < CHUNK:
        lvl.append(((ri // (2 * s)) == (ci // (2 * s))) & (((ri // s) % 2) == 1) & (((ci // s) % 2) == 0))
        s *= 2

    def tiles(ref):
        return jnp.stack([ref[p * P:(p + 1) * P, h * DN_DK:(h + 1) * DN_DK]
                          for p in range(npair) for h in range(H)])

    def untile(x):
        return jnp.concatenate([jnp.concatenate([x[p * H + h] for h in range(H)], axis=1)
                                for p in range(npair)], axis=0)

    q16 = tiles(q_ref)
    k16 = tiles(k_ref)
    qf = q16.astype(F32)
    kf = k16.astype(F32)
    vf = tiles(v_ref).astype(F32)
    bg = bg_ref[...]
    bcol = jnp.stack([bg[p * P:(p + 1) * P, h:h + 1] for p in range(npair) for h in range(H)])
    gcol = jnp.stack([bg[p * P:(p + 1) * P, H + h:H + h + 1] for p in range(npair) for h in range(H)])
    grow = jnp.sum(jnp.where(eye, gcol, 0.0), axis=1, keepdims=True)
    Gcol = jnp.sum(jnp.where(causal, grow, 0.0), axis=2, keepdims=True)
    Grow = jnp.sum(jnp.where(eye, Gcol, 0.0), axis=1, keepdims=True)
    Glast = jnp.sum(jnp.where(lastm, Grow, 0.0), axis=2, keepdims=True)
    D = jnp.where(causal, jnp.exp(jnp.where(causal, Gcol - Grow, 0.0)), 0.0)
    eG = jnp.exp(Gcol)
    kb = kf * bcol
    kbg = kb * eG
    qg = qf * eG
    kd = kf * jnp.exp(Glast - Gcol)
    vb = vf * bcol
    L = jnp.where(strict, _bmm_nt(kb.astype(BF16), k16) * D, 0.0)
    A = jnp.where(causal, _bmm_nt(q16, k16) * D, 0.0)
    N = -jnp.where(lvl[0], L, 0.0)
    for m in lvl[1:]:
        C = jnp.where(m, L, 0.0)
        Nb = N.astype(BF16)
        X = C + _bmm(C.astype(BF16), Nb)
        N = N - (X + _bmm(Nb, X.astype(BF16)))
    T = (jnp.where(eye, 1.0, 0.0) + N).astype(BF16)
    WU = _bmm(T, jnp.concatenate([kbg, vb], axis=2).astype(BF16))
    w_ref[...] = untile(WU[:, :, :DN_DK]).astype(BF16)
    u_ref[...] = untile(WU[:, :, DN_DK:])
    qg_ref[...] = untile(qg).astype(BF16)
    a_ref[...] = untile(A).astype(BF16)
    kdt_ref[...] = untile(jnp.swapaxes(kd, 1, 2)).astype(BF16)
    glf = jnp.broadcast_to(jnp.exp(Glast), (npair * H, P, DN_DV))
    gl_ref[...] = untile(jnp.concatenate([glf[:, c * CHUNK:c * CHUNK + SUBLANES, :]
                                          for c in range(P // CHUNK)], axis=1))


def _delta_scan_kernel(w_ref, qg_ref, u_ref, a_ref, kdt_ref, gl_ref, sz_ref, nw_ref, o_ref, s_ref, *, tc, nbat):
    P, H = PAIR, DN_HEADS

    @pl.when(pl.program_id(0) == 0)
    def _():
        s_ref[...] = jnp.zeros(s_ref.shape, F32)

    nw = nw_ref[...]

    def chains(ref, rows):
        return jnp.stack([ref[b, rows, h * DN_DV:(h + 1) * DN_DV] for b in range(nbat) for h in range(H)])

    def pair(p, carry):
        r0 = pl.multiple_of(p * P, P)
        kdt = chains(kdt_ref, pl.ds(r0, P))
        S = s_ref[...]
        for c in range(P // CHUNK):
            rows = pl.ds(pl.multiple_of(r0 + c * CHUNK, CHUNK), CHUNK)
            lhs1 = jnp.concatenate([chains(w_ref, rows), chains(qg_ref, rows)], axis=1)
            WS = _bmm(lhs1, S.astype(BF16))
            ub = (chains(u_ref, rows) - WS[:, :CHUNK]).astype(BF16)
            z = jnp.zeros_like(ub)
            rhs2 = jnp.concatenate([ub, z] if c == 0 else [z, ub], axis=1)
            R = _bmm(jnp.concatenate([chains(a_ref, rows), kdt], axis=1), rhs2)
            o = WS[:, CHUNK:] + R[:, :CHUNK]
            grow8 = pl.ds(pl.multiple_of(p * (P // SUBLANES) + c * SUBLANES, SUBLANES), SUBLANES)
            S = S * chains(gl_ref, grow8)[:, 0:1, :] + R[:, CHUNK:]
            o = o * lax.rsqrt(jnp.mean(o * o, axis=-1, keepdims=True) + RMS_EPS) * nw
            o = (o * chains(sz_ref, rows).astype(F32)).astype(BF16)
            o_ref[:, rows, :] = jnp.stack([jnp.concatenate([o[b * H + h] for h in range(H)], axis=1)
                                           for b in range(nbat)])
        s_ref[...] = S
        return carry

    lax.fori_loop(0, tc // P, pair, 0)


def _mix_out_kernel(x_ref, oa_ref, ob_ref, wg_ref, wa_ref, wb_ref, wo_ref, g_ref, b_ref, out_ref):
    x = x_ref[...]
    gates = jax.nn.sigmoid(_dot(x.astype(BF16), wg_ref[...]))
    ya = _dot(oa_ref[...], wa_ref[...])
    yb = _dot(ob_ref[...], wb_ref[...])
    h = gates[:, :D_MODEL] * ya + gates[:, D_MODEL:] * yb
    y = _dot(h.astype(BF16), wo_ref[...])
    out_ref[...] = _layer_norm(DEEPNORM_ALPHA * x + y, g_ref[...], b_ref[...])


def _ffn_kernel(x_ref, wup_ref, cw_ref, wdn_ref, g_ref, b_ref, out_ref, carry_ref, buf_ref, acc_ref, *, tm):
    halo = SUBLANES

    @pl.when(pl.program_id(1) == 0)
    def _():
        carry_ref[...] = jnp.zeros(carry_ref.shape, F32)

    x = x_ref[...]
    xb = x.astype(BF16)
    acc_ref[...] = jnp.zeros(acc_ref.shape, F32)

    def chunk(j, carry):
        p = _dot(xb, wup_ref[j])
        buf_ref[0:halo, :] = carry_ref[j]
        buf_ref[halo:halo + tm, :] = p
        cw = cw_ref[j]
        c = p * cw[FFN_CONV - 1:FFN_CONV, :]
        for t in range(FFN_CONV - 1):
            sh = FFN_CONV - 1 - t
            c = c + buf_ref[halo - sh:halo - sh + tm, :] * cw[t:t + 1, :]
        carry_ref[j] = p[tm - halo:, :]
        a = c[:, :FFN_CHUNK]
        hm = (a * jax.nn.sigmoid(a) * c[:, FFN_CHUNK:]).astype(BF16)
        acc_ref[...] += _dot(hm, wdn_ref[j])
        return carry

    lax.fori_loop(0, N_FFN_CHUNKS, chunk, 0)
    out_ref[...] = _layer_norm(DEEPNORM_ALPHA * x + acc_ref[...], g_ref[...], b_ref[...])


def _const_spec(shape):
    nd = len(shape)
    return pl.BlockSpec(shape, lambda *_: (0,) * nd)


def _params(sem):
    return pltpu.CompilerParams(dimension_semantics=sem, vmem_limit_bytes=VMEM_LIMIT)


def _layer(xf, B, S, p):
    T = B * S
    tm = TOKEN_TILE
    ns = S // tm
    seq = ("arbitrary", "arbitrary")

    def tok(width):
        return pl.BlockSpec((tm, width), lambda b, s: (b * ns + s, 0))

    q, k, v, sz, bg, ob = pl.pallas_call(
        functools.partial(_in_proj_kernel, tm=tm),
        grid=(B, ns),
        in_specs=[tok(D_MODEL), _const_spec(p["wq"].shape), _const_spec(p["wbd"].shape),
                  _const_spec(p["wuv"].shape), _const_spec(p["cw"].shape), _const_spec(p["alog"].shape),
                  _const_spec(p["dtb"].shape), _const_spec(p["lng"].shape), _const_spec(p["lnb"].shape),
                  _const_spec(p["ws"].shape), _const_spec(p["bst"].shape)],
        out_specs=[tok(QK_W), tok(QK_W), tok(V_W), tok(V_W), tok(2 * DN_HEADS), tok(SG_W)],
        out_shape=[jax.ShapeDtypeStruct((T, QK_W), BF16), jax.ShapeDtypeStruct((T, QK_W), BF16),
                   jax.ShapeDtypeStruct((T, V_W), BF16), jax.ShapeDtypeStruct((T, V_W), BF16),
                   jax.ShapeDtypeStruct((T, 2 * DN_HEADS), F32), jax.ShapeDtypeStruct((T, SG_W), BF16)],
        scratch_shapes=[pltpu.VMEM((tm + SUBLANES, QKV_W), F32)],
        compiler_params=_params(seq),
        name="in_proj",
    )(xf, p["wq"], p["wbd"], p["wuv"], p["cw"], p["alog"], p["dtb"], p["lng"], p["lnb"], p["ws"], p["bst"])

    tokv = tok(V_W)
    w, qg, u, a, kdt, gl = pl.pallas_call(
        functools.partial(_delta_prep_kernel, tt=tm),
        grid=(B, ns),
        in_specs=[tok(QK_W), tok(QK_W), tokv, tok(2 * DN_HEADS)],
        out_specs=[tokv, tokv, tokv, tokv, tokv,
                   pl.BlockSpec((tm // SUBLANES, V_W), lambda b, s: (b * ns + s, 0))],
        out_shape=[jax.ShapeDtypeStruct((T, V_W), BF16), jax.ShapeDtypeStruct((T, V_W), BF16),
                   jax.ShapeDtypeStruct((T, V_W), F32), jax.ShapeDtypeStruct((T, V_W), BF16),
                   jax.ShapeDtypeStruct((T, V_W), BF16), jax.ShapeDtypeStruct((T // SUBLANES, V_W), F32)],
        compiler_params=_params(seq),
        name="delta_prep",
    )(q, k, v, bg)

    def seq3(rows):
        return pl.BlockSpec((B, rows, V_W), lambda s: (0, s, 0))

    def b3(t):
        return t.reshape(B, t.shape[0] // B, V_W)

    oa = pl.pallas_call(
        functools.partial(_delta_scan_kernel, tc=tm, nbat=B),
        grid=(ns,),
        in_specs=[seq3(tm), seq3(tm), seq3(tm), seq3(tm), seq3(tm), seq3(tm // SUBLANES), seq3(tm),
                  _const_spec(p["nw"].shape)],
        out_specs=seq3(tm),
        out_shape=jax.ShapeDtypeStruct((B, S, V_W), BF16),
        scratch_shapes=[pltpu.VMEM((B * DN_HEADS, DN_DK, DN_DV), F32)],
        compiler_params=_params(("arbitrary",)),
        name="delta_scan",
    )(b3(w), b3(qg), b3(u), b3(a), b3(kdt), b3(gl), b3(sz), p["nw"]).reshape(T, V_W)

    def tok1(width):
        return pl.BlockSpec((tm, width), lambda i: (i, 0))

    x1 = pl.pallas_call(
        _mix_out_kernel,
        grid=(T // tm,),
        in_specs=[tok1(D_MODEL), tok1(V_W), tok1(SG_W), _const_spec(p["wg"].shape), _const_spec(p["wa"].shape),
                  _const_spec(p["wb"].shape), _const_spec(p["wo"].shape), _const_spec(p["ln1g"].shape),
                  _const_spec(p["ln1b"].shape)],
        out_specs=tok1(D_MODEL),
        out_shape=jax.ShapeDtypeStruct((T, D_MODEL), F32),
        compiler_params=_params(("arbitrary",)),
        name="mix_out",
    )(xf, oa, ob, p["wg"], p["wa"], p["wb"], p["wo"], p["ln1g"], p["ln1b"])

    x2 = pl.pallas_call(
        functools.partial(_ffn_kernel, tm=tm),
        grid=(B, ns),
        in_specs=[tok(D_MODEL), _const_spec(p["wup"].shape), _const_spec(p["cwf"].shape),
                  _const_spec(p["wdn"].shape), _const_spec(p["ln2g"].shape), _const_spec(p["ln2b"].shape)],
        out_specs=tok(D_MODEL),
        out_shape=jax.ShapeDtypeStruct((T, D_MODEL), F32),
        scratch_shapes=[pltpu.VMEM((N_FFN_CHUNKS, SUBLANES, 2 * FFN_CHUNK), F32),
                        pltpu.VMEM((tm + SUBLANES, 2 * FFN_CHUNK), F32),
                        pltpu.VMEM((tm, D_MODEL), F32)],
        compiler_params=_params(seq),
        name="ffn",
    )(x1, p["wup"], p["cwf"], p["wdn"], p["ln2g"], p["ln2b"])
    return x2


def _pad_rows(a, rows):
    return jnp.pad(a, ((0, rows - a.shape[0]), (0, 0)))


def _lane_row(vals, offset):
    return jnp.zeros((1, LANES), F32).at[0, offset:offset + vals.shape[0]].set(vals)


def _layer_params(l, w_in, conv_qkv, a_log, dt_bias, dn_norm_w, w_branch_a, sg_ln_g, sg_ln_b, w_spatial,
                  b_spatial, w_branch_b, w_out, ln1_g, ln1_b, w_up, conv_ffn, w_down, ln2_g, ln2_b):
    wi = w_in[l]
    nc = N_FFN_CHUNKS

    def pair_chunks(a):
        lead = a.shape[:-1]
        a = a.reshape(lead + (2, nc, FFN_CHUNK))
        a = jnp.moveaxis(a, -2, 0)
        return a.reshape((nc,) + lead + (2 * FFN_CHUNK,))

    return dict(
        wq=wi[:, :QKVZ_W].astype(BF16),
        wbd=jnp.pad(wi[:, BD_OFF:UV_OFF], ((0, 0), (0, LANES - 2 * DN_HEADS))).astype(BF16),
        wuv=wi[:, UV_OFF:GATE_OFF].astype(BF16),
        wg=wi[:, GATE_OFF:].astype(BF16),
        cw=_pad_rows(conv_qkv[l], SUBLANES),
        alog=_lane_row(a_log[l], DN_HEADS),
        dtb=_lane_row(dt_bias[l], DN_HEADS),
        nw=dn_norm_w[l].reshape(1, DN_DV),
        lng=sg_ln_g[l].reshape(1, SG_W),
        lnb=sg_ln_b[l].reshape(1, SG_W),
        ws=w_spatial[l],
        bst=b_spatial[l].T,
        wa=w_branch_a[l].astype(BF16),
        wb=w_branch_b[l].astype(BF16),
        wo=w_out[l].astype(BF16),
        ln1g=ln1_g[l].reshape(1, D_MODEL),
        ln1b=ln1_b[l].reshape(1, D_MODEL),
        wup=pair_chunks(w_up[l]).astype(BF16),
        cwf=jnp.pad(pair_chunks(conv_ffn[l]), ((0, 0), (0, SUBLANES - FFN_CONV), (0, 0))),
        wdn=w_down[l].reshape(nc, FFN_CHUNK, D_MODEL).astype(BF16),
        ln2g=ln2_g[l].reshape(1, D_MODEL),
        ln2b=ln2_b[l].reshape(1, D_MODEL),
    )


def kernel(x, w_in, conv_qkv, a_log, dt_bias, dn_norm_w, w_branch_a, sg_ln_g, sg_ln_b, w_spatial, b_spatial, w_branch_b, w_out, ln1_g, ln1_b, w_up, conv_ffn, w_down, ln2_g, ln2_b):
    B, S, D = x.shape
    assert D == D_MODEL and S % TOKEN_TILE == 0
    xf = x.reshape(B * S, D)
    for l in range(w_in.shape[0]):
        p = _layer_params(l, w_in, conv_qkv, a_log, dt_bias, dn_norm_w, w_branch_a, sg_ln_g, sg_ln_b,
                          w_spatial, b_spatial, w_branch_b, w_out, ln1_g, ln1_b, w_up, conv_ffn, w_down,
                          ln2_g, ln2_b)
        xf = _layer(xf, B, S, p)
    return xf.reshape(B, S, D)
```

```python
import functools

import jax
import jax.numpy as jnp
from jax import lax
from jax.experimental import pallas as pl
from jax.experimental.pallas import tpu as pltpu

F32 = jnp.float32
BF16 = jnp.bfloat16

D_MODEL = 1024
DEPTH = 2
CHUNK = 64
DN_HEADS = 4
DN_DK = 128
DN_DV = 128
DN_CONV = 4
SG_GROUPS = 4
SG_GROUP_DIM = 128
SG_BLOCK = 128
FFN_DIM = 2816
FFN_CONV = 3
LN_EPS = 1e-5
RMS_EPS = 1e-6
L2_EPS = 1e-6
DEEPNORM_ALPHA = (2 * DEPTH) ** 0.25

QK_W = DN_HEADS * DN_DK
V_W = DN_HEADS * DN_DV
SG_W = SG_GROUPS * SG_GROUP_DIM
QKV_W = 2 * QK_W + V_W
QKVZ_W = QKV_W + V_W
BD_OFF = QKVZ_W
UV_OFF = BD_OFF + 2 * DN_HEADS
GATE_OFF = UV_OFF + 2 * SG_W

LANES = 128
SUBLANES = 8
PAIR = 2 * CHUNK
FFN_CHUNK = 256
N_FFN_CHUNKS = FFN_DIM // FFN_CHUNK
FFN_CONV_SLOTS = 4
TOKEN_TILE = 512
FFN_TILE = 512
SUB_TILE = 512
VMEM_LIMIT = 56 * 1024 * 1024


def _layer_norm(r, g, b):
    mu = jnp.mean(r, axis=-1, keepdims=True)
    d = r - mu
    var = jnp.mean(d * d, axis=-1, keepdims=True)
    return d * lax.rsqrt(var + LN_EPS) * g + b


def _dot(a, b):
    return jnp.dot(a, b, preferred_element_type=F32)


def _dot_nt(a, b):
    return lax.dot_general(a, b, (((1,), (1,)), ((), ())), preferred_element_type=F32)


def _dot_tn(a, b):
    return lax.dot_general(a, b, (((0,), (0,)), ((), ())), preferred_element_type=F32)


def _in_proj_kernel(x_ref, wq_ref, wbd_ref, wuv_ref, cw_ref, alog_ref, dtb_ref, lng_ref, lnb_ref,
                    ws_ref, bst_ref, q_ref, k_ref, v_ref, sz_ref, bg_ref, ob_ref, cbuf_ref, *, tm):
    halo = SUBLANES

    @pl.when(pl.program_id(1) == 0)
    def _():
        cbuf_ref[0:halo, :] = jnp.zeros((halo, QKV_W), F32)

    xb = x_ref[...].astype(BF16)
    proj = _dot(xb, wq_ref[...])
    pre = proj[:, :QKV_W]
    cbuf_ref[halo:halo + tm, :] = pre
    acc = pre * cw_ref[DN_CONV - 1:DN_CONV, :]
    for j in range(DN_CONV - 1):
        sh = DN_CONV - 1 - j
        acc = acc + cbuf_ref[halo - sh:halo - sh + tm, :] * cw_ref[j:j + 1, :]
    cbuf_ref[0:halo, :] = cbuf_ref[tm:tm + halo, :]
    qkv = acc * jax.nn.sigmoid(acc)
    for h in range(DN_HEADS):
        sl = slice(h * DN_DK, (h + 1) * DN_DK)
        qh = qkv[:, sl]
        qn = qh * lax.rsqrt(jnp.sum(qh * qh, axis=-1, keepdims=True) + L2_EPS) * (DN_DK ** -0.5)
        q_ref[:, sl] = qn.astype(BF16)
        kh = qkv[:, QK_W + h * DN_DK:QK_W + (h + 1) * DN_DK]
        kn = kh * lax.rsqrt(jnp.sum(kh * kh, axis=-1, keepdims=True) + L2_EPS)
        k_ref[:, sl] = kn.astype(BF16)
    v_ref[...] = qkv[:, 2 * QK_W:].astype(BF16)
    z = proj[:, QKV_W:]
    sz_ref[...] = (z * jax.nn.sigmoid(z)).astype(BF16)

    bd = _dot(xb, wbd_ref[...])
    beta = jax.nn.sigmoid(bd)
    a = bd + dtb_ref[...]
    g = -jnp.exp(alog_ref[...]) * (jnp.maximum(a, 0.0) + jnp.log1p(jnp.exp(-jnp.abs(a))))
    lane = lax.broadcasted_iota(jnp.int32, bd.shape, 1)
    bg_ref[...] = jnp.where(lane < DN_HEADS, beta, g)[:, :2 * DN_HEADS]

    uv = _dot(xb, wuv_ref[...])
    ge = uv * (0.5 * (1.0 + jnp.tanh(0.7978845608028654 * (uv + 0.044715 * (uv * uv * uv)))))
    u = ge[:, :SG_W]
    vln = _layer_norm(ge[:, SG_W:], lng_ref[...], lnb_ref[...]).astype(BF16)
    ri = lax.broadcasted_iota(jnp.int32, (SG_BLOCK, SG_BLOCK), 0)
    ci = lax.broadcasted_iota(jnp.int32, (SG_BLOCK, SG_BLOCK), 1)
    nb = tm // SG_BLOCK
    for gi in range(SG_GROUPS):
        cs = slice(gi * SG_GROUP_DIM, (gi + 1) * SG_GROUP_DIM)
        wm = jnp.where(ci <= ri, ws_ref[gi], 0.0).astype(BF16)
        rhs = jnp.concatenate([vln[r * SG_BLOCK:(r + 1) * SG_BLOCK, cs] for r in range(nb)], axis=1)
        m = _dot(wm, rhs) + bst_ref[:, gi:gi + 1]
        for r in range(nb):
            rs = slice(r * SG_BLOCK, (r + 1) * SG_BLOCK)
            ob_ref[rs, cs] = (u[rs, cs] * m[:, r * SG_GROUP_DIM:(r + 1) * SG_GROUP_DIM]).astype(BF16)


def _bmm(a, b):
    return lax.dot_general(a, b, (((2,), (1,)), ((0,), (0,))), preferred_element_type=F32)


def _bmm_nt(a, b):
    return lax.dot_general(a, b, (((2,), (2,)), ((0,), (0,))), preferred_element_type=F32)


def _delta_prep_kernel(q_ref, k_ref, v_ref, bg_ref, w_ref, qg_ref, u_ref, a_ref, kdt_ref, gl_ref, *, tt):
    P, H = PAIR, DN_HEADS
    npair = tt // P
    ri = lax.broadcasted_iota(jnp.int32, (P, P), 0)
    ci = lax.broadcasted_iota(jnp.int32, (P, P), 1)
    same = (ri // CHUNK) == (ci // CHUNK)
    eye = ri == ci
    causal = same & (ci <= ri)
    strict = same & (ci < ri)
    lastm = ci == (ri // CHUNK) * CHUNK + (CHUNK - 1)
    lvl = []
    s = 1
    while s < CHUNK:
        lvl.append(((ri // (2 * s)) == (ci // (2 * s))) & (((ri // s) % 2) == 1) & (((ci // s) % 2) == 0))
        s *= 2

    def tiles(ref):
        return jnp.stack([ref[p * P:(p + 1) * P, h * DN_DK:(h + 1) * DN_DK]
                          for p in range(npair) for h in range(H)])

    def untile(x):
        return jnp.concatenate([jnp.concatenate([x[p * H + h] for h in range(H)], axis=1)
                                for p in range(npair)], axis=0)

    q16 = tiles(q_ref)
    k16 = tiles(k_ref)
    qf = q16.astype(F32)
    kf = k16.astype(F32)
    vf = tiles(v_ref).astype(F32)
    bg = bg_ref[...]
    bcol = jnp.stack([bg[p * P:(p + 1) * P, h:h + 1] for p in range(npair) for h in range(H)])
    gcol = jnp.stack([bg[p * P:(p + 1) * P, H + h:H + h + 1] for p in range(npair) for h in range(H)])
    grow = jnp.sum(jnp.where(eye, gcol, 0.0), axis=1, keepdims=True)
    Gcol = jnp.sum(jnp.where(causal, grow, 0.0), axis=2, keepdims=True)
    Grow = jnp.sum(jnp.where(eye, Gcol, 0.0), axis=1, keepdims=True)
    Glast = jnp.sum(jnp.where(lastm, Grow, 0.0), axis=2, keepdims=True)
    D = jnp.where(causal, jnp.exp(jnp.where(causal, Gcol - Grow, 0.0)), 0.0)
    eG = jnp.exp(Gcol)
    kb = kf * bcol
    kbg = kb * eG
    qg = qf * eG
    kd = kf * jnp.exp(Glast - Gcol)
    vb = vf * bcol
    L = jnp.where(strict, _bmm_nt(kb.astype(BF16), k16) * D, 0.0)
    A = jnp.where(causal, _bmm_nt(q16, k16) * D, 0.0)
    N = -jnp.where(lvl[0], L, 0.0)
    for m in lvl[1:]:
        C = jnp.where(m, L, 0.0)
        Nb = N.astype(BF16)
        X = C + _bmm(C.astype(BF16), Nb)
        N = N - (X + _bmm(Nb, X.astype(BF16)))
    T = (jnp.where(eye, 1.0, 0.0) + N).astype(BF16)
    WU = _bmm(T, jnp.concatenate([kbg, vb], axis=2).astype(BF16))
    w_ref[...] = untile(WU[:, :, :DN_DK]).astype(BF16)
    u_ref[...] = untile(WU[:, :, DN_DK:])
    qg_ref[...] = untile(qg).astype(BF16)
    a_ref[...] = untile(A).astype(BF16)
    kdt_ref[...] = untile(jnp.swapaxes(kd, 1, 2)).astype(BF16)
    glf = jnp.broadcast_to(jnp.exp(Glast), (npair * H, P, DN_DV))
    gl_ref[...] = untile(jnp.concatenate([glf[:, c * CHUNK:c * CHUNK + SUBLANES, :]
                                          for c in range(P // CHUNK)], axis=1))


def _delta_scan_kernel(w_ref, qg_ref, u_ref, a_ref, kdt_ref, gl_ref, sz_ref, nw_ref, o_ref, s_ref, *, tc, nbat):
    P, H = PAIR, DN_HEADS

    @pl.when(pl.program_id(0) == 0)
    def _():
        s_ref[...] = jnp.zeros(s_ref.shape, F32)

    nw = nw_ref[...]

    def chains(ref, rows):
        return jnp.stack([ref[b, rows, h * DN_DV:(h + 1) * DN_DV] for b in range(nbat) for h in range(H)])

    def pair(p, carry):
        r0 = pl.multiple_of(p * P, P)
        kdt = chains(kdt_ref, pl.ds(r0, P))
        S = s_ref[...]
        for c in range(P // CHUNK):
            rows = pl.ds(pl.multiple_of(r0 + c * CHUNK, CHUNK), CHUNK)
            lhs1 = jnp.concatenate([chains(w_ref, rows), chains(qg_ref, rows)], axis=1)
            WS = _bmm(lhs1, S.astype(BF16))
            ub = (chains(u_ref, rows) - WS[:, :CHUNK]).astype(BF16)
            z = jnp.zeros_like(ub)
            rhs2 = jnp.concatenate([ub, z] if c == 0 else [z, ub], axis=1)
            R = _bmm(jnp.concatenate([chains(a_ref, rows), kdt], axis=1), rhs2)
            o = WS[:, CHUNK:] + R[:, :CHUNK]
            grow8 = pl.ds(pl.multiple_of(p * (P // SUBLANES) + c * SUBLANES, SUBLANES), SUBLANES)
            S = S * chains(gl_ref, grow8)[:, 0:1, :] + R[:, CHUNK:]
            o = o * lax.rsqrt(jnp.mean(o * o, axis=-1, keepdims=True) + RMS_EPS) * nw
            o = (o * chains(sz_ref, rows).astype(F32)).astype(BF16)
            o_ref[:, rows, :] = jnp.stack([jnp.concatenate([o[b * H + h] for h in range(H)], axis=1)
                                           for b in range(nbat)])
        s_ref[...] = S
        return carry

    lax.fori_loop(0, tc // P, pair, 0)


def _mix_out_kernel(x_ref, oa_ref, ob_ref, wg_ref, wa_ref, wb_ref, wo_ref, g_ref, b_ref, out_ref):
    x = x_ref[...]
    gates = jax.nn.sigmoid(_dot(x.astype(BF16), wg_ref[...]))
    ya = _dot(oa_ref[...], wa_ref[...])
    yb = _dot(ob_ref[...], wb_ref[...])
    h = gates[:, :D_MODEL] * ya + gates[:, D_MODEL:] * yb
    y = _dot(h.astype(BF16), wo_ref[...])
    out_ref[...] = _layer_norm(DEEPNORM_ALPHA * x + y, g_ref[...], b_ref[...])


def _ffn_kernel(x_ref, wup_ref, cw_ref, wdn_ref, g_ref, b_ref, out_ref, carry_ref, buf_ref, hm_ref, xb_ref, *, tm, sub):
    halo = SUBLANES

    @pl.when(pl.program_id(1) == 0)
    def _():
        carry_ref[...] = jnp.zeros(carry_ref.shape, F32)

    nslot = buf_ref.shape[0]

    def cols(j):
        return [slice(h * FFN_DIM + j * FFN_CHUNK, h * FFN_DIM + (j + 1) * FFN_CHUNK) for h in range(2)]

    for r in range(tm // sub):
        rows = slice(r * sub, (r + 1) * sub)
        xb_ref[rows, :] = x_ref[rows, :].astype(BF16)

        def up(j):
            return [_dot(xb_ref[rows, :], wup_ref[:, cs]) for cs in cols(j)]

        def conv_half(i, p, cs):
            slot = buf_ref.at[i % nslot]
            slot[0:halo, :] = carry_ref[:, cs]
            slot[halo:halo + sub, :] = p
            c = p * cw_ref[FFN_CONV - 1:FFN_CONV, cs]
            for t in range(FFN_CONV - 1):
                sh = FFN_CONV - 1 - t
                c = c + slot[halo - sh:halo - sh + sub, :] * cw_ref[t:t + 1, cs]
            carry_ref[:, cs] = p[sub - halo:, :]
            return c

        ps = up(0)
        for j in range(N_FFN_CHUNKS):
            ps_next = up(j + 1) if j + 1 < N_FFN_CHUNKS else None
            a, b = [conv_half(2 * (r * N_FFN_CHUNKS + j) + h, ps[h], cs) for h, cs in enumerate(cols(j))]
            ps = ps_next
            hm_ref[rows, j * FFN_CHUNK:(j + 1) * FFN_CHUNK] = (a * jax.nn.sigmoid(a) * b).astype(BF16)
        y = _dot(hm_ref[rows, :], wdn_ref[...])
        out_ref[rows, :] = _layer_norm(DEEPNORM_ALPHA * x_ref[rows, :] + y, g_ref[...], b_ref[...])


def _const_spec(shape):
    nd = len(shape)
    return pl.BlockSpec(shape, lambda *_: (0,) * nd)


def _params(sem):
    return pltpu.CompilerParams(dimension_semantics=sem, vmem_limit_bytes=VMEM_LIMIT)


def _layer(xf, B, S, p):
    T = B * S
    tm = TOKEN_TILE
    ns = S // tm
    seq = ("arbitrary", "arbitrary")

    def tok(width):
        return pl.BlockSpec((tm, width), lambda b, s: (b * ns + s, 0))

    q, k, v, sz, bg, ob = pl.pallas_call(
        functools.partial(_in_proj_kernel, tm=tm),
        grid=(B, ns),
        in_specs=[tok(D_MODEL), _const_spec(p["wq"].shape), _const_spec(p["wbd"].shape),
                  _const_spec(p["wuv"].shape), _const_spec(p["cw"].shape), _const_spec(p["alog"].shape),
                  _const_spec(p["dtb"].shape), _const_spec(p["lng"].shape), _const_spec(p["lnb"].shape),
                  _const_spec(p["ws"].shape), _const_spec(p["bst"].shape)],
        out_specs=[tok(QK_W), tok(QK_W), tok(V_W), tok(V_W), tok(2 * DN_HEADS), tok(SG_W)],
        out_shape=[jax.ShapeDtypeStruct((T, QK_W), BF16), jax.ShapeDtypeStruct((T, QK_W), BF16),
                   jax.ShapeDtypeStruct((T, V_W), BF16), jax.ShapeDtypeStruct((T, V_W), BF16),
                   jax.ShapeDtypeStruct((T, 2 * DN_HEADS), F32), jax.ShapeDtypeStruct((T, SG_W), BF16)],
        scratch_shapes=[pltpu.VMEM((tm + SUBLANES, QKV_W), F32)],
        compiler_params=_params(seq),
        name="in_proj",
    )(xf, p["wq"], p["wbd"], p["wuv"], p["cw"], p["alog"], p["dtb"], p["lng"], p["lnb"], p["ws"], p["bst"])

    tokv = tok(V_W)
    w, qg, u, a, kdt, gl = pl.pallas_call(
        functools.partial(_delta_prep_kernel, tt=tm),
        grid=(B, ns),
        in_specs=[tok(QK_W), tok(QK_W), tokv, tok(2 * DN_HEADS)],
        out_specs=[tokv, tokv, tokv, tokv, tokv,
                   pl.BlockSpec((tm // SUBLANES, V_W), lambda b, s: (b * ns + s, 0))],
        out_shape=[jax.ShapeDtypeStruct((T, V_W), BF16), jax.ShapeDtypeStruct((T, V_W), BF16),
                   jax.ShapeDtypeStruct((T, V_W), F32), jax.ShapeDtypeStruct((T, V_W), BF16),
                   jax.ShapeDtypeStruct((T, V_W), BF16), jax.ShapeDtypeStruct((T // SUBLANES, V_W), F32)],
        compiler_params=_params(seq),
        name="delta_prep",
    )(q, k, v, bg)

    def seq3(rows):
        return pl.BlockSpec((B, rows, V_W), lambda s: (0, s, 0))

    def b3(t):
        return t.reshape(B, t.shape[0] // B, V_W)

    oa = pl.pallas_call(
        functools.partial(_delta_scan_kernel, tc=tm, nbat=B),
        grid=(ns,),
        in_specs=[seq3(tm), seq3(tm), seq3(tm), seq3(tm), seq3(tm), seq3(tm // SUBLANES), seq3(tm),
                  _const_spec(p["nw"].shape)],
        out_specs=seq3(tm),
        out_shape=jax.ShapeDtypeStruct((B, S, V_W), BF16),
        scratch_shapes=[pltpu.VMEM((B * DN_HEADS, DN_DK, DN_DV), F32)],
        compiler_params=_params(("arbitrary",)),
        name="delta_scan",
    )(b3(w), b3(qg), b3(u), b3(a), b3(kdt), b3(gl), b3(sz), p["nw"]).reshape(T, V_W)

    def tok1(width):
        return pl.BlockSpec((tm, width), lambda i: (i, 0))

    x1 = pl.pallas_call(
        _mix_out_kernel,
        grid=(T // tm,),
        in_specs=[tok1(D_MODEL), tok1(V_W), tok1(SG_W), _const_spec(p["wg"].shape), _const_spec(p["wa"].shape),
                  _const_spec(p["wb"].shape), _const_spec(p["wo"].shape), _const_spec(p["ln1g"].shape),
                  _const_spec(p["ln1b"].shape)],
        out_specs=tok1(D_MODEL),
        out_shape=jax.ShapeDtypeStruct((T, D_MODEL), F32),
        compiler_params=_params(("arbitrary",)),
        name="mix_out",
    )(xf, oa, ob, p["wg"], p["wa"], p["wb"], p["wo"], p["ln1g"], p["ln1b"])

    tf = FFN_TILE
    nf = S // tf
    tokf = pl.BlockSpec((tf, D_MODEL), lambda b, s: (b * nf + s, 0))
    x2 = pl.pallas_call(
        functools.partial(_ffn_kernel, tm=tf, sub=SUB_TILE),
        grid=(B, nf),
        in_specs=[tokf, _const_spec(p["wup"].shape), _const_spec(p["cwf"].shape),
                  _const_spec(p["wdn"].shape), _const_spec(p["ln2g"].shape), _const_spec(p["ln2b"].shape)],
        out_specs=tokf,
        out_shape=jax.ShapeDtypeStruct((T, D_MODEL), F32),
        scratch_shapes=[pltpu.VMEM((SUBLANES, 2 * FFN_DIM), F32),
                        pltpu.VMEM((FFN_CONV_SLOTS, SUB_TILE + SUBLANES, FFN_CHUNK), F32),
                        pltpu.VMEM((tf, FFN_DIM), BF16),
                        pltpu.VMEM((tf, D_MODEL), BF16)],
        compiler_params=_params(seq),
        name="ffn",
    )(x1, p["wup"], p["cwf"], p["wdn"], p["ln2g"], p["ln2b"])
    return x2


def _pad_rows(a, rows):
    return jnp.pad(a, ((0, rows - a.shape[0]), (0, 0)))


def _lane_row(vals, offset):
    return jnp.zeros((1, LANES), F32).at[0, offset:offset + vals.shape[0]].set(vals)


def _layer_params(l, w_in, conv_qkv, a_log, dt_bias, dn_norm_w, w_branch_a, sg_ln_g, sg_ln_b, w_spatial,
                  b_spatial, w_branch_b, w_out, ln1_g, ln1_b, w_up, conv_ffn, w_down, ln2_g, ln2_b):
    wi = w_in[l]
    return dict(
        wq=wi[:, :QKVZ_W].astype(BF16),
        wbd=jnp.pad(wi[:, BD_OFF:UV_OFF], ((0, 0), (0, LANES - 2 * DN_HEADS))).astype(BF16),
        wuv=wi[:, UV_OFF:GATE_OFF].astype(BF16),
        wg=wi[:, GATE_OFF:].astype(BF16),
        cw=_pad_rows(conv_qkv[l], SUBLANES),
        alog=_lane_row(a_log[l], DN_HEADS),
        dtb=_lane_row(dt_bias[l], DN_HEADS),
        nw=dn_norm_w[l].reshape(1, DN_DV),
        lng=sg_ln_g[l].reshape(1, SG_W),
        lnb=sg_ln_b[l].reshape(1, SG_W),
        ws=w_spatial[l],
        bst=b_spatial[l].T,
        wa=w_branch_a[l].astype(BF16),
        wb=w_branch_b[l].astype(BF16),
        wo=w_out[l].astype(BF16),
        ln1g=ln1_g[l].reshape(1, D_MODEL),
        ln1b=ln1_b[l].reshape(1, D_MODEL),
        wup=w_up[l].astype(BF16),
        cwf=_pad_rows(conv_ffn[l], SUBLANES),
        wdn=w_down[l].astype(BF16),
        ln2g=ln2_g[l].reshape(1, D_MODEL),
        ln2b=ln2_b[l].reshape(1, D_MODEL),
    )


def kernel(x, w_in, conv_qkv, a_log, dt_bias, dn_norm_w, w_branch_a, sg_ln_g, sg_ln_b, w_spatial, b_spatial, w_branch_b, w_out, ln1_g, ln1_b, w_up, conv_ffn, w_down, ln2_g, ln2_b):
    B, S, D = x.shape
    assert D == D_MODEL and S % TOKEN_TILE == 0
    xf = x.reshape(B * S, D)
    for l in range(w_in.shape[0]):
        p = _layer_params(l, w_in, conv_qkv, a_log, dt_bias, dn_norm_w, w_branch_a, sg_ln_g, sg_ln_b,
                          w_spatial, b_spatial, w_branch_b, w_out, ln1_g, ln1_b, w_up, conv_ffn, w_down,
                          ln2_g, ln2_b)
        xf = _layer(xf, B, S, p)
    return xf.reshape(B, S, D)
```

```python
import functools

import jax
import jax.numpy as jnp
from jax import lax
from jax.experimental import pallas as pl
from jax.experimental.pallas import tpu as pltpu

F32 = jnp.float32
BF16 = jnp.bfloat16

D_MODEL = 1024
DEPTH = 2
CHUNK = 64
DN_HEADS = 4
DN_DK = 128
DN_DV = 128
DN_CONV = 4
SG_GROUPS = 4
SG_GROUP_DIM = 128
SG_BLOCK = 128
FFN_DIM = 2816
FFN_CONV = 3
LN_EPS = 1e-5
RMS_EPS = 1e-6
L2_EPS = 1e-6
DEEPNORM_ALPHA = (2 * DEPTH) ** 0.25

QK_W = DN_HEADS * DN_DK
V_W = DN_HEADS * DN_DV
SG_W = SG_GROUPS * SG_GROUP_DIM
QKV_W = 2 * QK_W + V_W
QKVZ_W = QKV_W + V_W
BD_OFF = QKVZ_W
UV_OFF = BD_OFF + 2 * DN_HEADS
GATE_OFF = UV_OFF + 2 * SG_W

LANES = 128
SUBLANES = 8
PAIR = 2 * CHUNK
FFN_CHUNK = 256
N_FFN_CHUNKS = FFN_DIM // FFN_CHUNK
FFN_CONV_SLOTS = 4
TOKEN_TILE = 512
FFN_TILE = 512
DELTA_TILE = 256
DELTA_GROUP = 2
SUB_TILE = 512
VMEM_LIMIT = 56 * 1024 * 1024


def _layer_norm(r, g, b):
    mu = jnp.mean(r, axis=-1, keepdims=True)
    d = r - mu
    var = jnp.mean(d * d, axis=-1, keepdims=True)
    return d * lax.rsqrt(var + LN_EPS) * g + b


def _dot(a, b):
    return jnp.dot(a, b, preferred_element_type=F32)


def _dot_nt(a, b):
    return lax.dot_general(a, b, (((1,), (1,)), ((), ())), preferred_element_type=F32)


def _dot_tn(a, b):
    return lax.dot_general(a, b, (((0,), (0,)), ((), ())), preferred_element_type=F32)


def _in_proj_kernel(x_ref, wq_ref, wbd_ref, wuv_ref, cw_ref, alog_ref, dtb_ref, lng_ref, lnb_ref,
                    ws_ref, bst_ref, q_ref, k_ref, v_ref, sz_ref, bg_ref, ob_ref, cbuf_ref, *, tm):
    halo = SUBLANES

    @pl.when(pl.program_id(1) == 0)
    def _():
        cbuf_ref[0:halo, :] = jnp.zeros((halo, QKV_W), F32)

    xb = x_ref[...].astype(BF16)
    proj = _dot(xb, wq_ref[...])
    pre = proj[:, :QKV_W]
    cbuf_ref[halo:halo + tm, :] = pre
    acc = pre * cw_ref[DN_CONV - 1:DN_CONV, :]
    for j in range(DN_CONV - 1):
        sh = DN_CONV - 1 - j
        acc = acc + cbuf_ref[halo - sh:halo - sh + tm, :] * cw_ref[j:j + 1, :]
    cbuf_ref[0:halo, :] = cbuf_ref[tm:tm + halo, :]
    qkv = acc * jax.nn.sigmoid(acc)
    for h in range(DN_HEADS):
        sl = slice(h * DN_DK, (h + 1) * DN_DK)
        qh = qkv[:, sl]
        qn = qh * lax.rsqrt(jnp.sum(qh * qh, axis=-1, keepdims=True) + L2_EPS) * (DN_DK ** -0.5)
        q_ref[:, sl] = qn.astype(BF16)
        kh = qkv[:, QK_W + h * DN_DK:QK_W + (h + 1) * DN_DK]
        kn = kh * lax.rsqrt(jnp.sum(kh * kh, axis=-1, keepdims=True) + L2_EPS)
        k_ref[:, sl] = kn.astype(BF16)
    v_ref[...] = qkv[:, 2 * QK_W:].astype(BF16)
    z = proj[:, QKV_W:]
    sz_ref[...] = (z * jax.nn.sigmoid(z)).astype(BF16)

    bd = _dot(xb, wbd_ref[...])
    beta = jax.nn.sigmoid(bd)
    a = bd + dtb_ref[...]
    g = -jnp.exp(alog_ref[...]) * (jnp.maximum(a, 0.0) + jnp.log1p(jnp.exp(-jnp.abs(a))))
    lane = lax.broadcasted_iota(jnp.int32, bd.shape, 1)
    bg_ref[...] = jnp.where(lane < DN_HEADS, beta, g)

    uv = _dot(xb, wuv_ref[...])
    ge = uv * (0.5 * (1.0 + jnp.tanh(0.7978845608028654 * (uv + 0.044715 * (uv * uv * uv)))))
    u = ge[:, :SG_W]
    vln = _layer_norm(ge[:, SG_W:], lng_ref[...], lnb_ref[...]).astype(BF16)
    ri = lax.broadcasted_iota(jnp.int32, (SG_BLOCK, SG_BLOCK), 0)
    ci = lax.broadcasted_iota(jnp.int32, (SG_BLOCK, SG_BLOCK), 1)
    nb = tm // SG_BLOCK
    for gi in range(SG_GROUPS):
        cs = slice(gi * SG_GROUP_DIM, (gi + 1) * SG_GROUP_DIM)
        wm = jnp.where(ci <= ri, ws_ref[gi], 0.0).astype(BF16)
        rhs = jnp.concatenate([vln[r * SG_BLOCK:(r + 1) * SG_BLOCK, cs] for r in range(nb)], axis=1)
        m = _dot(wm, rhs) + bst_ref[:, gi:gi + 1]
        for r in range(nb):
            rs = slice(r * SG_BLOCK, (r + 1) * SG_BLOCK)
            ob_ref[rs, cs] = (u[rs, cs] * m[:, r * SG_GROUP_DIM:(r + 1) * SG_GROUP_DIM]).astype(BF16)


def _bmm(a, b):
    return lax.dot_general(a, b, (((2,), (1,)), ((0,), (0,))), preferred_element_type=F32)


def _bmm_nt(a, b):
    return lax.dot_general(a, b, (((2,), (2,)), ((0,), (0,))), preferred_element_type=F32)


def _delta_prep(q_refs, k_refs, v_refs, bg_refs, w_refs, qg_refs, u_refs, a_refs, kdt_refs, gl_refs, tt):
    P, H = PAIR, DN_HEADS
    npair = tt // P
    ri = lax.broadcasted_iota(jnp.int32, (P, P), 0)
    ci = lax.broadcasted_iota(jnp.int32, (P, P), 1)
    same = (ri // CHUNK) == (ci // CHUNK)
    eye = ri == ci
    causal = same & (ci <= ri)
    strict = same & (ci < ri)
    lastm = ci == (ri // CHUNK) * CHUNK + (CHUNK - 1)
    lvl = []
    s = 1
    while s < CHUNK:
        lvl.append(((ri // (2 * s)) == (ci // (2 * s))) & (((ri // s) % 2) == 1) & (((ci // s) % 2) == 0))
        s *= 2

    nseq = len(q_refs)

    def tiles(refs):
        return jnp.stack([ref[p * P:(p + 1) * P, h * DN_DK:(h + 1) * DN_DK]
                          for ref in refs for p in range(npair) for h in range(H)])

    def untile(x, i):
        base = i * npair * H
        return jnp.concatenate([jnp.concatenate([x[base + p * H + h] for h in range(H)], axis=1)
                                for p in range(npair)], axis=0)

    q16 = tiles(q_refs)
    k16 = tiles(k_refs)
    qf = q16.astype(F32)
    kf = k16.astype(F32)
    vf = tiles(v_refs).astype(F32)
    bgs = [ref[...] for ref in bg_refs]
    bcol = jnp.stack([bg[p * P:(p + 1) * P, h:h + 1] for bg in bgs for p in range(npair) for h in range(H)])
    gcol = jnp.stack([bg[p * P:(p + 1) * P, H + h:H + h + 1]
                      for bg in bgs for p in range(npair) for h in range(H)])
    grow = jnp.sum(jnp.where(eye, gcol, 0.0), axis=1, keepdims=True)
    Gcol = jnp.sum(jnp.where(causal, grow, 0.0), axis=2, keepdims=True)
    Grow = jnp.sum(jnp.where(eye, Gcol, 0.0), axis=1, keepdims=True)
    Glast = jnp.sum(jnp.where(lastm, Grow, 0.0), axis=2, keepdims=True)
    D = jnp.where(causal, jnp.exp(jnp.where(causal, Gcol - Grow, 0.0)), 0.0)
    eG = jnp.exp(Gcol)
    kb = kf * bcol
    kbg = kb * eG
    qg = qf * eG
    kd = kf * jnp.exp(Glast - Gcol)
    vb = vf * bcol
    L = jnp.where(strict, _bmm_nt(kb.astype(BF16), k16) * D, 0.0)
    A = jnp.where(causal, _bmm_nt(q16, k16) * D, 0.0)
    N = -jnp.where(lvl[0], L, 0.0)
    for m in lvl[1:]:
        C = jnp.where(m, L, 0.0)
        Nb = N.astype(BF16)
        X = C + _bmm(C.astype(BF16), Nb)
        N = N - (X + _bmm(Nb, X.astype(BF16)))
    T = (jnp.where(eye, 1.0, 0.0) + N).astype(BF16)
    WU = _bmm(T, jnp.concatenate([kbg, vb], axis=2).astype(BF16))
    kdt = jnp.swapaxes(kd, 1, 2)
    glf = jnp.broadcast_to(jnp.exp(Glast), (nseq * npair * H, P, DN_DV))
    gl8 = jnp.concatenate([glf[:, c * CHUNK:c * CHUNK + SUBLANES, :] for c in range(P // CHUNK)], axis=1)
    for i in range(nseq):
        w_refs[i][...] = untile(WU[:, :, :DN_DK], i).astype(BF16)
        u_refs[i][...] = untile(WU[:, :, DN_DK:], i)
        qg_refs[i][...] = untile(qg, i).astype(BF16)
        a_refs[i][...] = untile(A, i).astype(BF16)
        kdt_refs[i][...] = untile(kdt, i).astype(BF16)
        gl_refs[i][...] = untile(gl8, i)


def _delta_kernel(q_ref, k_ref, v_ref, bg_ref, sz_ref, nw_ref, o_ref,
                  s_ref, w_ref, qg_ref, u_ref, a_ref, kdt_ref, gl_ref, *, tc, nbat, group):
    P, H = PAIR, DN_HEADS

    @pl.when(pl.program_id(0) == 0)
    def _():
        s_ref[...] = jnp.zeros(s_ref.shape, F32)

    for g0 in range(0, nbat, group):
        bs = range(g0, g0 + group)
        _delta_prep([q_ref.at[b] for b in bs], [k_ref.at[b] for b in bs], [v_ref.at[b] for b in bs],
                    [bg_ref.at[b] for b in bs], [w_ref.at[b] for b in bs], [qg_ref.at[b] for b in bs],
                    [u_ref.at[b] for b in bs], [a_ref.at[b] for b in bs], [kdt_ref.at[b] for b in bs],
                    [gl_ref.at[b] for b in bs], tc)

    nw = nw_ref[...]

    def chains(ref, rows):
        return jnp.stack([ref[b, rows, h * DN_DV:(h + 1) * DN_DV] for b in range(nbat) for h in range(H)])

    def pair(p, carry):
        r0 = pl.multiple_of(p * P, P)
        kdt = chains(kdt_ref, pl.ds(r0, P))
        S = s_ref[...]
        for c in range(P // CHUNK):
            rows = pl.ds(pl.multiple_of(r0 + c * CHUNK, CHUNK), CHUNK)
            lhs1 = jnp.concatenate([chains(w_ref, rows), chains(qg_ref, rows)], axis=1)
            WS = _bmm(lhs1, S.astype(BF16))
            ub = (chains(u_ref, rows) - WS[:, :CHUNK]).astype(BF16)
            z = jnp.zeros_like(ub)
            rhs2 = jnp.concatenate([ub, z] if c == 0 else [z, ub], axis=1)
            R = _bmm(jnp.concatenate([chains(a_ref, rows), kdt], axis=1), rhs2)
            o = WS[:, CHUNK:] + R[:, :CHUNK]
            grow8 = pl.ds(pl.multiple_of(p * (P // SUBLANES) + c * SUBLANES, SUBLANES), SUBLANES)
            S = S * chains(gl_ref, grow8)[:, 0:1, :] + R[:, CHUNK:]
            o = o * lax.rsqrt(jnp.mean(o * o, axis=-1, keepdims=True) + RMS_EPS) * nw
            o = (o * chains(sz_ref, rows).astype(F32)).astype(BF16)
            o_ref[:, rows, :] = jnp.stack([jnp.concatenate([o[b * H + h] for h in range(H)], axis=1)
                                           for b in range(nbat)])
        s_ref[...] = S
        return carry

    lax.fori_loop(0, tc // P, pair, 0)


def _mix_out_kernel(x_ref, oa_ref, ob_ref, wg_ref, wa_ref, wb_ref, wo_ref, g_ref, b_ref, out_ref):
    x = x_ref[...]
    gates = jax.nn.sigmoid(_dot(x.astype(BF16), wg_ref[...]))
    ya = _dot(oa_ref[...], wa_ref[...])
    yb = _dot(ob_ref[...], wb_ref[...])
    h = gates[:, :D_MODEL] * ya + gates[:, D_MODEL:] * yb
    y = _dot(h.astype(BF16), wo_ref[...])
    out_ref[...] = _layer_norm(DEEPNORM_ALPHA * x + y, g_ref[...], b_ref[...])


def _ffn_kernel(x_ref, wup_ref, cw_ref, wdn_ref, g_ref, b_ref, out_ref, carry_ref, buf_ref, hm_ref, xb_ref, *, tm, sub):
    halo = SUBLANES

    @pl.when(pl.program_id(1) == 0)
    def _():
        carry_ref[...] = jnp.zeros(carry_ref.shape, F32)

    nslot = buf_ref.shape[0]

    def cols(j):
        return [slice(h * FFN_DIM + j * FFN_CHUNK, h * FFN_DIM + (j + 1) * FFN_CHUNK) for h in range(2)]

    for r in range(tm // sub):
        rows = slice(r * sub, (r + 1) * sub)
        xb_ref[rows, :] = x_ref[rows, :].astype(BF16)

        def up(j):
            return [_dot(xb_ref[rows, :], wup_ref[:, cs]) for cs in cols(j)]

        def conv_half(i, p, cs):
            slot = buf_ref.at[i % nslot]
            slot[0:halo, :] = carry_ref[:, cs]
            slot[halo:halo + sub, :] = p
            c = p * cw_ref[FFN_CONV - 1:FFN_CONV, cs]
            for t in range(FFN_CONV - 1):
                sh = FFN_CONV - 1 - t
                c = c + slot[halo - sh:halo - sh + sub, :] * cw_ref[t:t + 1, cs]
            carry_ref[:, cs] = p[sub - halo:, :]
            return c

        ps = up(0)
        for j in range(N_FFN_CHUNKS):
            ps_next = up(j + 1) if j + 1 < N_FFN_CHUNKS else None
            a, b = [conv_half(2 * (r * N_FFN_CHUNKS + j) + h, ps[h], cs) for h, cs in enumerate(cols(j))]
            ps = ps_next
            hm_ref[rows, j * FFN_CHUNK:(j + 1) * FFN_CHUNK] = (a * jax.nn.sigmoid(a) * b).astype(BF16)
        y = _dot(hm_ref[rows, :], wdn_ref[...])
        out_ref[rows, :] = _layer_norm(DEEPNORM_ALPHA * x_ref[rows, :] + y, g_ref[...], b_ref[...])


def _const_spec(shape):
    nd = len(shape)
    return pl.BlockSpec(shape, lambda *_: (0,) * nd)


def _params(sem):
    return pltpu.CompilerParams(dimension_semantics=sem, vmem_limit_bytes=VMEM_LIMIT)


def _layer(xf, B, S, p):
    T = B * S
    tm = TOKEN_TILE
    ns = S // tm
    seq = ("arbitrary", "arbitrary")

    def tok(width):
        return pl.BlockSpec((tm, width), lambda b, s: (b * ns + s, 0))

    q, k, v, sz, bg, ob = pl.pallas_call(
        functools.partial(_in_proj_kernel, tm=tm),
        grid=(B, ns),
        in_specs=[tok(D_MODEL), _const_spec(p["wq"].shape), _const_spec(p["wbd"].shape),
                  _const_spec(p["wuv"].shape), _const_spec(p["cw"].shape), _const_spec(p["alog"].shape),
                  _const_spec(p["dtb"].shape), _const_spec(p["lng"].shape), _const_spec(p["lnb"].shape),
                  _const_spec(p["ws"].shape), _const_spec(p["bst"].shape)],
        out_specs=[tok(QK_W), tok(QK_W), tok(V_W), tok(V_W), tok(LANES), tok(SG_W)],
        out_shape=[jax.ShapeDtypeStruct((T, QK_W), BF16), jax.ShapeDtypeStruct((T, QK_W), BF16),
                   jax.ShapeDtypeStruct((T, V_W), BF16), jax.ShapeDtypeStruct((T, V_W), BF16),
                   jax.ShapeDtypeStruct((T, LANES), F32), jax.ShapeDtypeStruct((T, SG_W), BF16)],
        scratch_shapes=[pltpu.VMEM((tm + SUBLANES, QKV_W), F32)],
        compiler_params=_params(seq),
        name="in_proj",
    )(xf, p["wq"], p["wbd"], p["wuv"], p["cw"], p["alog"], p["dtb"], p["lng"], p["lnb"], p["ws"], p["bst"])

    tc = DELTA_TILE

    def seq3(width):
        return pl.BlockSpec((B, tc, width), lambda s: (0, s, 0))

    def b3(t):
        return t.reshape(B, S, t.shape[-1])

    def scr(rows, dtype):
        return pltpu.VMEM((B, rows, V_W), dtype)

    oa = pl.pallas_call(
        functools.partial(_delta_kernel, tc=tc, nbat=B, group=DELTA_GROUP),
        grid=(S // tc,),
        in_specs=[seq3(QK_W), seq3(QK_W), seq3(V_W), seq3(LANES), seq3(V_W), _const_spec(p["nw"].shape)],
        out_specs=seq3(V_W),
        out_shape=jax.ShapeDtypeStruct((B, S, V_W), BF16),
        scratch_shapes=[pltpu.VMEM((B * DN_HEADS, DN_DK, DN_DV), F32),
                        scr(tc, BF16), scr(tc, BF16), scr(tc, F32), scr(tc, BF16), scr(tc, BF16),
                        scr(tc // SUBLANES, F32)],
        compiler_params=_params(("arbitrary",)),
        name="delta",
    )(b3(q), b3(k), b3(v), b3(bg), b3(sz), p["nw"]).reshape(T, V_W)

    def tok1(width):
        return pl.BlockSpec((tm, width), lambda i: (i, 0))

    x1 = pl.pallas_call(
        _mix_out_kernel,
        grid=(T // tm,),
        in_specs=[tok1(D_MODEL), tok1(V_W), tok1(SG_W), _const_spec(p["wg"].shape), _const_spec(p["wa"].shape),
                  _const_spec(p["wb"].shape), _const_spec(p["wo"].shape), _const_spec(p["ln1g"].shape),
                  _const_spec(p["ln1b"].shape)],
        out_specs=tok1(D_MODEL),
        out_shape=jax.ShapeDtypeStruct((T, D_MODEL), F32),
        compiler_params=_params(("arbitrary",)),
        name="mix_out",
    )(xf, oa, ob, p["wg"], p["wa"], p["wb"], p["wo"], p["ln1g"], p["ln1b"])

    tf = FFN_TILE
    nf = S // tf
    tokf = pl.BlockSpec((tf, D_MODEL), lambda b, s: (b * nf + s, 0))
    x2 = pl.pallas_call(
        functools.partial(_ffn_kernel, tm=tf, sub=SUB_TILE),
        grid=(B, nf),
        in_specs=[tokf, _const_spec(p["wup"].shape), _const_spec(p["cwf"].shape),
                  _const_spec(p["wdn"].shape), _const_spec(p["ln2g"].shape), _const_spec(p["ln2b"].shape)],
        out_specs=tokf,
        out_shape=jax.ShapeDtypeStruct((T, D_MODEL), F32),
        scratch_shapes=[pltpu.VMEM((SUBLANES, 2 * FFN_DIM), F32),
                        pltpu.VMEM((FFN_CONV_SLOTS, SUB_TILE + SUBLANES, FFN_CHUNK), F32),
                        pltpu.VMEM((tf, FFN_DIM), BF16),
                        pltpu.VMEM((tf, D_MODEL), BF16)],
        compiler_params=_params(seq),
        name="ffn",
    )(x1, p["wup"], p["cwf"], p["wdn"], p["ln2g"], p["ln2b"])
    return x2


def _pad_rows(a, rows):
    return jnp.pad(a, ((0, rows - a.shape[0]), (0, 0)))


def _lane_row(vals, offset):
    return jnp.zeros((1, LANES), F32).at[0, offset:offset + vals.shape[0]].set(vals)


def _layer_params(l, w_in, conv_qkv, a_log, dt_bias, dn_norm_w, w_branch_a, sg_ln_g, sg_ln_b, w_spatial,
                  b_spatial, w_branch_b, w_out, ln1_g, ln1_b, w_up, conv_ffn, w_down, ln2_g, ln2_b):
    wi = w_in[l]
    return dict(
        wq=wi[:, :QKVZ_W].astype(BF16),
        wbd=jnp.pad(wi[:, BD_OFF:UV_OFF], ((0, 0), (0, LANES - 2 * DN_HEADS))).astype(BF16),
        wuv=wi[:, UV_OFF:GATE_OFF].astype(BF16),
        wg=wi[:, GATE_OFF:].astype(BF16),
        cw=_pad_rows(conv_qkv[l], SUBLANES),
        alog=_lane_row(a_log[l], DN_HEADS),
        dtb=_lane_row(dt_bias[l], DN_HEADS),
        nw=dn_norm_w[l].reshape(1, DN_DV),
        lng=sg_ln_g[l].reshape(1, SG_W),
        lnb=sg_ln_b[l].reshape(1, SG_W),
        ws=w_spatial[l],
        bst=b_spatial[l].T,
        wa=w_branch_a[l].astype(BF16),
        wb=w_branch_b[l].astype(BF16),
        wo=w_out[l].astype(BF16),
        ln1g=ln1_g[l].reshape(1, D_MODEL),
        ln1b=ln1_b[l].reshape(1, D_MODEL),
        wup=w_up[l].astype(BF16),
        cwf=_pad_rows(conv_ffn[l], SUBLANES),
        wdn=w_down[l].astype(BF16),
        ln2g=ln2_g[l].reshape(1, D_MODEL),
        ln2b=ln2_b[l].reshape(1, D_MODEL),
    )


def kernel(x, w_in, conv_qkv, a_log, dt_bias, dn_norm_w, w_branch_a, sg_ln_g, sg_ln_b, w_spatial, b_spatial, w_branch_b, w_out, ln1_g, ln1_b, w_up, conv_ffn, w_down, ln2_g, ln2_b):
    B, S, D = x.shape
    assert D == D_MODEL and S % TOKEN_TILE == 0 and B % DELTA_GROUP == 0
    xf = x.reshape(B * S, D)
    for l in range(w_in.shape[0]):
        p = _layer_params(l, w_in, conv_qkv, a_log, dt_bias, dn_norm_w, w_branch_a, sg_ln_g, sg_ln_b,
                          w_spatial, b_spatial, w_branch_b, w_out, ln1_g, ln1_b, w_up, conv_ffn, w_down,
                          ln2_g, ln2_b)
        xf = _layer(xf, B, S, p)
    return xf.reshape(B, S, D)
```

```python
import functools

import jax
import jax.numpy as jnp
from jax import lax
from jax.experimental import pallas as pl
from jax.experimental.pallas import tpu as pltpu

F32 = jnp.float32
BF16 = jnp.bfloat16

D_MODEL = 1024
DEPTH = 2
CHUNK = 64
DN_HEADS = 4
DN_DK = 128
DN_DV = 128
DN_CONV = 4
SG_GROUPS = 4
SG_GROUP_DIM = 128
SG_BLOCK = 128
FFN_DIM = 2816
FFN_CONV = 3
LN_EPS = 1e-5
RMS_EPS = 1e-6
L2_EPS = 1e-6
DEEPNORM_ALPHA = (2 * DEPTH) ** 0.25

QK_W = DN_HEADS * DN_DK
V_W = DN_HEADS * DN_DV
SG_W = SG_GROUPS * SG_GROUP_DIM
QKV_W = 2 * QK_W + V_W
QKVZ_W = QKV_W + V_W
BD_OFF = QKVZ_W
UV_OFF = BD_OFF + 2 * DN_HEADS
GATE_OFF = UV_OFF + 2 * SG_W

LANES = 128
SUBLANES = 8
PAIR = 2 * CHUNK
FFN_CHUNK = 256
N_FFN_CHUNKS = FFN_DIM // FFN_CHUNK
FFN_CONV_SLOTS = 4
TOKEN_TILE = 1024
FFN_TILE = 512
IN_TILE = 1024
ACT_BLOCK = 512
FFN_TAIL_CHUNKS = 4
MIX_TAIL_CHUNKS = 4
DELTA_TILE = 256
DELTA_GROUP = 2
SUB_TILE = 512
VMEM_LIMIT = 56 * 1024 * 1024


def _layer_norm(r, g, b):
    mu = jnp.mean(r, axis=-1, keepdims=True)
    d = r - mu
    var = jnp.mean(d * d, axis=-1, keepdims=True)
    return d * lax.rsqrt(var + LN_EPS) * g + b


def _dot(a, b):
    return jnp.dot(a, b, preferred_element_type=F32)


def _dot_nt(a, b):
    return lax.dot_general(a, b, (((1,), (1,)), ((), ())), preferred_element_type=F32)


def _dot_tn(a, b):
    return lax.dot_general(a, b, (((0,), (0,)), ((), ())), preferred_element_type=F32)


def _in_proj_kernel(x_ref, wq_ref, wbd_ref, wuv_ref, cw_ref, alog_ref, dtb_ref, lng_ref, lnb_ref,
                    ws_ref, bst_ref, act_ref, bg_ref, cbuf_ref, *, tm):
    halo = SUBLANES
    q_ref, k_ref, v_ref, sz_ref, ob_ref = [act_ref.at[:, i * ACT_BLOCK:(i + 1) * ACT_BLOCK] for i in range(5)]

    @pl.when(pl.program_id(1) == 0)
    def _():
        cbuf_ref[0:halo, :] = jnp.zeros((halo, QKV_W), F32)

    xb = x_ref[...].astype(BF16)
    proj = _dot(xb, wq_ref[...])
    pre = proj[:, :QKV_W]
    cbuf_ref[halo:halo + tm, :] = pre
    acc = pre * cw_ref[DN_CONV - 1:DN_CONV, :]
    for j in range(DN_CONV - 1):
        sh = DN_CONV - 1 - j
        acc = acc + cbuf_ref[halo - sh:halo - sh + tm, :] * cw_ref[j:j + 1, :]
    cbuf_ref[0:halo, :] = cbuf_ref[tm:tm + halo, :]
    qkv = acc * jax.nn.sigmoid(acc)
    for h in range(DN_HEADS):
        sl = slice(h * DN_DK, (h + 1) * DN_DK)
        qh = qkv[:, sl]
        qn = qh * lax.rsqrt(jnp.sum(qh * qh, axis=-1, keepdims=True) + L2_EPS) * (DN_DK ** -0.5)
        q_ref[:, sl] = qn.astype(BF16)
        kh = qkv[:, QK_W + h * DN_DK:QK_W + (h + 1) * DN_DK]
        kn = kh * lax.rsqrt(jnp.sum(kh * kh, axis=-1, keepdims=True) + L2_EPS)
        k_ref[:, sl] = kn.astype(BF16)
    v_ref[...] = qkv[:, 2 * QK_W:].astype(BF16)
    z = proj[:, QKV_W:]
    sz_ref[...] = (z * jax.nn.sigmoid(z)).astype(BF16)

    bd = _dot(xb, wbd_ref[...])
    beta = jax.nn.sigmoid(bd)
    a = bd + dtb_ref[...]
    g = -jnp.exp(alog_ref[...]) * (jnp.maximum(a, 0.0) + jnp.log1p(jnp.exp(-jnp.abs(a))))
    lane = lax.broadcasted_iota(jnp.int32, bd.shape, 1)
    bg_ref[...] = jnp.where(lane < DN_HEADS, beta, g)

    uv = _dot(xb, wuv_ref[...])
    ge = uv * (0.5 * (1.0 + jnp.tanh(0.7978845608028654 * (uv + 0.044715 * (uv * uv * uv)))))
    u = ge[:, :SG_W]
    vln = _layer_norm(ge[:, SG_W:], lng_ref[...], lnb_ref[...]).astype(BF16)
    ri = lax.broadcasted_iota(jnp.int32, (SG_BLOCK, SG_BLOCK), 0)
    ci = lax.broadcasted_iota(jnp.int32, (SG_BLOCK, SG_BLOCK), 1)
    nb = tm // SG_BLOCK
    for gi in range(SG_GROUPS):
        cs = slice(gi * SG_GROUP_DIM, (gi + 1) * SG_GROUP_DIM)
        wm = jnp.where(ci <= ri, ws_ref[gi], 0.0).astype(BF16)
        rhs = jnp.concatenate([vln[r * SG_BLOCK:(r + 1) * SG_BLOCK, cs] for r in range(nb)], axis=1)
        m = _dot(wm, rhs) + bst_ref[:, gi:gi + 1]
        for r in range(nb):
            rs = slice(r * SG_BLOCK, (r + 1) * SG_BLOCK)
            ob_ref[rs, cs] = (u[rs, cs] * m[:, r * SG_GROUP_DIM:(r + 1) * SG_GROUP_DIM]).astype(BF16)


def _bmm(a, b):
    return lax.dot_general(a, b, (((2,), (1,)), ((0,), (0,))), preferred_element_type=F32)


def _bmm_nt(a, b):
    return lax.dot_general(a, b, (((2,), (2,)), ((0,), (0,))), preferred_element_type=F32)


def _delta_prep(q_refs, k_refs, v_refs, bg_refs, w_refs, qg_refs, u_refs, a_refs, kdt_refs, gl_refs, tt):
    P, H = PAIR, DN_HEADS
    npair = tt // P
    ri = lax.broadcasted_iota(jnp.int32, (P, P), 0)
    ci = lax.broadcasted_iota(jnp.int32, (P, P), 1)
    same = (ri // CHUNK) == (ci // CHUNK)
    eye = ri == ci
    causal = same & (ci <= ri)
    strict = same & (ci < ri)
    lastm = ci == (ri // CHUNK) * CHUNK + (CHUNK - 1)
    lvl = []
    s = 1
    while s < CHUNK:
        lvl.append(((ri // (2 * s)) == (ci // (2 * s))) & (((ri // s) % 2) == 1) & (((ci // s) % 2) == 0))
        s *= 2

    nseq = len(q_refs)

    def tiles(refs):
        return jnp.stack([ref[p * P:(p + 1) * P, h * DN_DK:(h + 1) * DN_DK]
                          for ref in refs for p in range(npair) for h in range(H)])

    def untile(x, i):
        base = i * npair * H
        return jnp.concatenate([jnp.concatenate([x[base + p * H + h] for h in range(H)], axis=1)
                                for p in range(npair)], axis=0)

    q16 = tiles(q_refs)
    k16 = tiles(k_refs)
    qf = q16.astype(F32)
    kf = k16.astype(F32)
    vf = tiles(v_refs).astype(F32)
    bgs = [ref[...] for ref in bg_refs]
    bcol = jnp.stack([bg[p * P:(p + 1) * P, h:h + 1] for bg in bgs for p in range(npair) for h in range(H)])
    gcol = jnp.stack([bg[p * P:(p + 1) * P, H + h:H + h + 1]
                      for bg in bgs for p in range(npair) for h in range(H)])
    grow = jnp.sum(jnp.where(eye, gcol, 0.0), axis=1, keepdims=True)
    Gcol = jnp.sum(jnp.where(causal, grow, 0.0), axis=2, keepdims=True)
    Grow = jnp.sum(jnp.where(eye, Gcol, 0.0), axis=1, keepdims=True)
    Glast = jnp.sum(jnp.where(lastm, Grow, 0.0), axis=2, keepdims=True)
    D = jnp.where(causal, jnp.exp(jnp.where(causal, Gcol - Grow, 0.0)), 0.0)
    eG = jnp.exp(Gcol)
    kb = kf * bcol
    kbg = kb * eG
    qg = qf * eG
    kd = kf * jnp.exp(Glast - Gcol)
    vb = vf * bcol
    L = jnp.where(strict, _bmm_nt(kb.astype(BF16), k16) * D, 0.0)
    A = jnp.where(causal, _bmm_nt(q16, k16) * D, 0.0)
    N = -jnp.where(lvl[0], L, 0.0)
    for m in lvl[1:]:
        C = jnp.where(m, L, 0.0)
        Nb = N.astype(BF16)
        X = C + _bmm(C.astype(BF16), Nb)
        N = N - (X + _bmm(Nb, X.astype(BF16)))
    T = (jnp.where(eye, 1.0, 0.0) + N).astype(BF16)
    WU = _bmm(T, jnp.concatenate([kbg, vb], axis=2).astype(BF16))
    kdt = jnp.swapaxes(kd, 1, 2)
    glf = jnp.broadcast_to(jnp.exp(Glast), (nseq * npair * H, P, DN_DV))
    gl8 = jnp.concatenate([glf[:, c * CHUNK:c * CHUNK + SUBLANES, :] for c in range(P // CHUNK)], axis=1)
    for i in range(nseq):
        w_refs[i][...] = untile(WU[:, :, :DN_DK], i).astype(BF16)
        u_refs[i][...] = untile(WU[:, :, DN_DK:], i)
        qg_refs[i][...] = untile(qg, i).astype(BF16)
        a_refs[i][...] = untile(A, i).astype(BF16)
        kdt_refs[i][...] = untile(kdt, i).astype(BF16)
        gl_refs[i][...] = untile(gl8, i)


def _delta_kernel(q_ref, k_ref, v_ref, bg_ref, sz_ref, nw_ref, o_ref,
                  s_ref, w_ref, qg_ref, u_ref, a_ref, kdt_ref, gl_ref, *, tc, nbat, group):
    P, H = PAIR, DN_HEADS

    @pl.when(pl.program_id(0) == 0)
    def _():
        s_ref[...] = jnp.zeros(s_ref.shape, F32)

    for g0 in range(0, nbat, group):
        bs = range(g0, g0 + group)
        _delta_prep([q_ref.at[b] for b in bs], [k_ref.at[b] for b in bs], [v_ref.at[b] for b in bs],
                    [bg_ref.at[b] for b in bs], [w_ref.at[b] for b in bs], [qg_ref.at[b] for b in bs],
                    [u_ref.at[b] for b in bs], [a_ref.at[b] for b in bs], [kdt_ref.at[b] for b in bs],
                    [gl_ref.at[b] for b in bs], tc)

    nw = nw_ref[...]

    def chains(ref, rows):
        return jnp.stack([ref[b, rows, h * DN_DV:(h + 1) * DN_DV] for b in range(nbat) for h in range(H)])

    def pair(p, carry):
        r0 = pl.multiple_of(p * P, P)
        kdt = chains(kdt_ref, pl.ds(r0, P))
        S = s_ref[...]
        for c in range(P // CHUNK):
            rows = pl.ds(pl.multiple_of(r0 + c * CHUNK, CHUNK), CHUNK)
            lhs1 = jnp.concatenate([chains(w_ref, rows), chains(qg_ref, rows)], axis=1)
            WS = _bmm(lhs1, S.astype(BF16))
            ub = (chains(u_ref, rows) - WS[:, :CHUNK]).astype(BF16)
            z = jnp.zeros_like(ub)
            rhs2 = jnp.concatenate([ub, z] if c == 0 else [z, ub], axis=1)
            R = _bmm(jnp.concatenate([chains(a_ref, rows), kdt], axis=1), rhs2)
            o = WS[:, CHUNK:] + R[:, :CHUNK]
            grow8 = pl.ds(pl.multiple_of(p * (P // SUBLANES) + c * SUBLANES, SUBLANES), SUBLANES)
            S = S * chains(gl_ref, grow8)[:, 0:1, :] + R[:, CHUNK:]
            o = o * lax.rsqrt(jnp.mean(o * o, axis=-1, keepdims=True) + RMS_EPS) * nw
            o = (o * chains(sz_ref, rows).astype(F32)).astype(BF16)
            o_ref[:, rows, :] = jnp.stack([jnp.concatenate([o[b * H + h] for h in range(H)], axis=1)
                                           for b in range(nbat)])
        s_ref[...] = S
        return carry

    lax.fori_loop(0, tc // P, pair, 0)


def _mix_out_kernel(x_ref, oa_ref, ob_ref, wg_ref, wa_ref, wb_ref, wo_ref, g_ref, b_ref, out_ref):
    x = x_ref[...]
    gates = jax.nn.sigmoid(_dot(x.astype(BF16), wg_ref[...]))
    ya = _dot(oa_ref[...], wa_ref[...])
    yb = _dot(ob_ref[...], wb_ref[...])
    h = gates[:, :D_MODEL] * ya + gates[:, D_MODEL:] * yb
    hb = h.astype(BF16)
    rt = x.shape[0] // MIX_TAIL_CHUNKS
    for c in range(MIX_TAIL_CHUNKS):
        rr = slice(c * rt, (c + 1) * rt)
        y = _dot(hb[rr], wo_ref[...])
        out_ref[rr, :] = _layer_norm(DEEPNORM_ALPHA * x[rr] + y, g_ref[...], b_ref[...])


def _ffn_kernel(x_ref, wup_ref, cw_ref, wdn_ref, g_ref, b_ref, out_ref, carry_ref, buf_ref, hm_ref, xb_ref, *, tm, sub):
    halo = SUBLANES

    @pl.when(pl.program_id(1) == 0)
    def _():
        carry_ref[...] = jnp.zeros(carry_ref.shape, F32)

    nslot = buf_ref.shape[0]

    def cols(j):
        return [slice(h * FFN_DIM + j * FFN_CHUNK, h * FFN_DIM + (j + 1) * FFN_CHUNK) for h in range(2)]

    for r in range(tm // sub):
        rows = slice(r * sub, (r + 1) * sub)
        xb_ref[rows, :] = x_ref[rows, :].astype(BF16)

        def up(j):
            return [_dot(xb_ref[rows, :], wup_ref[:, cs]) for cs in cols(j)]

        def conv_half(i, p, cs):
            slot = buf_ref.at[i % nslot]
            slot[0:halo, :] = carry_ref[:, cs]
            slot[halo:halo + sub, :] = p
            c = p * cw_ref[FFN_CONV - 1:FFN_CONV, cs]
            for t in range(FFN_CONV - 1):
                sh = FFN_CONV - 1 - t
                c = c + slot[halo - sh:halo - sh + sub, :] * cw_ref[t:t + 1, cs]
            carry_ref[:, cs] = p[sub - halo:, :]
            return c

        ps = up(0)
        for j in range(N_FFN_CHUNKS):
            ps_next = up(j + 1) if j + 1 < N_FFN_CHUNKS else None
            a, b = [conv_half(2 * (r * N_FFN_CHUNKS + j) + h, ps[h], cs) for h, cs in enumerate(cols(j))]
            ps = ps_next
            hm_ref[rows, j * FFN_CHUNK:(j + 1) * FFN_CHUNK] = (a * jax.nn.sigmoid(a) * b).astype(BF16)
        rt = sub // FFN_TAIL_CHUNKS
        for c in range(FFN_TAIL_CHUNKS):
            rr = slice(r * sub + c * rt, r * sub + (c + 1) * rt)
            y = _dot(hm_ref[rr, :], wdn_ref[...])
            out_ref[rr, :] = _layer_norm(DEEPNORM_ALPHA * x_ref[rr, :] + y, g_ref[...], b_ref[...])


def _const_spec(shape):
    nd = len(shape)
    return pl.BlockSpec(shape, lambda *_: (0,) * nd)


def _params(sem):
    return pltpu.CompilerParams(dimension_semantics=sem, vmem_limit_bytes=VMEM_LIMIT)


def _layer(xf, B, S, p):
    T = B * S
    tm = TOKEN_TILE
    ns = S // tm
    seq = ("arbitrary", "arbitrary")

    def tok(width):
        return pl.BlockSpec((tm, width), lambda b, s: (b * ns + s, 0))

    ti = IN_TILE
    ni = S // ti

    def toki(width):
        return pl.BlockSpec((ti, width), lambda b, s: (b * ni + s, 0))

    act, bg = pl.pallas_call(
        functools.partial(_in_proj_kernel, tm=ti),
        grid=(B, ni),
        in_specs=[toki(D_MODEL), _const_spec(p["wq"].shape), _const_spec(p["wbd"].shape),
                  _const_spec(p["wuv"].shape), _const_spec(p["cw"].shape), _const_spec(p["alog"].shape),
                  _const_spec(p["dtb"].shape), _const_spec(p["lng"].shape), _const_spec(p["lnb"].shape),
                  _const_spec(p["ws"].shape), _const_spec(p["bst"].shape)],
        out_specs=[toki(5 * ACT_BLOCK), toki(LANES)],
        out_shape=[jax.ShapeDtypeStruct((T, 5 * ACT_BLOCK), BF16), jax.ShapeDtypeStruct((T, LANES), F32)],
        scratch_shapes=[pltpu.VMEM((ti + SUBLANES, QKV_W), F32)],
        compiler_params=_params(seq),
        name="in_proj",
    )(xf, p["wq"], p["wbd"], p["wuv"], p["cw"], p["alog"], p["dtb"], p["lng"], p["lnb"], p["ws"], p["bst"])

    tc = DELTA_TILE

    def seq3(width):
        return pl.BlockSpec((B, tc, width), lambda s: (0, s, 0))

    def actcol(j):
        return pl.BlockSpec((B, tc, ACT_BLOCK), lambda s: (0, s, j))

    def b3(t):
        return t.reshape(B, S, t.shape[-1])

    def scr(rows, dtype):
        return pltpu.VMEM((B, rows, V_W), dtype)

    oa = pl.pallas_call(
        functools.partial(_delta_kernel, tc=tc, nbat=B, group=DELTA_GROUP),
        grid=(S // tc,),
        in_specs=[actcol(0), actcol(1), actcol(2), seq3(LANES), actcol(3), _const_spec(p["nw"].shape)],
        out_specs=seq3(V_W),
        out_shape=jax.ShapeDtypeStruct((B, S, V_W), BF16),
        scratch_shapes=[pltpu.VMEM((B * DN_HEADS, DN_DK, DN_DV), F32),
                        scr(tc, BF16), scr(tc, BF16), scr(tc, F32), scr(tc, BF16), scr(tc, BF16),
                        scr(tc // SUBLANES, F32)],
        compiler_params=_params(("arbitrary",)),
        name="delta",
    )(b3(act), b3(act), b3(act), b3(bg), b3(act), p["nw"]).reshape(T, V_W)

    def tok1(width):
        return pl.BlockSpec((tm, width), lambda i: (i, 0))

    x1 = pl.pallas_call(
        _mix_out_kernel,
        grid=(T // tm,),
        in_specs=[tok1(D_MODEL), tok1(V_W), pl.BlockSpec((tm, ACT_BLOCK), lambda i: (i, 4)), _const_spec(p["wg"].shape), _const_spec(p["wa"].shape),
                  _const_spec(p["wb"].shape), _const_spec(p["wo"].shape), _const_spec(p["ln1g"].shape),
                  _const_spec(p["ln1b"].shape)],
        out_specs=tok1(D_MODEL),
        out_shape=jax.ShapeDtypeStruct((T, D_MODEL), F32),
        compiler_params=_params(("arbitrary",)),
        name="mix_out",
    )(xf, oa, act, p["wg"], p["wa"], p["wb"], p["wo"], p["ln1g"], p["ln1b"])

    tf = FFN_TILE
    nf = S // tf
    tokf = pl.BlockSpec((tf, D_MODEL), lambda b, s: (b * nf + s, 0))
    x2 = pl.pallas_call(
        functools.partial(_ffn_kernel, tm=tf, sub=SUB_TILE),
        grid=(B, nf),
        in_specs=[tokf, _const_spec(p["wup"].shape), _const_spec(p["cwf"].shape),
                  _const_spec(p["wdn"].shape), _const_spec(p["ln2g"].shape), _const_spec(p["ln2b"].shape)],
        out_specs=tokf,
        out_shape=jax.ShapeDtypeStruct((T, D_MODEL), F32),
        scratch_shapes=[pltpu.VMEM((SUBLANES, 2 * FFN_DIM), F32),
                        pltpu.VMEM((FFN_CONV_SLOTS, SUB_TILE + SUBLANES, FFN_CHUNK), F32),
                        pltpu.VMEM((tf, FFN_DIM), BF16),
                        pltpu.VMEM((tf, D_MODEL), BF16)],
        compiler_params=_params(seq),
        name="ffn",
    )(x1, p["wup"], p["cwf"], p["wdn"], p["ln2g"], p["ln2b"])
    return x2


def _pad_rows(a, rows):
    return jnp.pad(a, ((0, rows - a.shape[0]), (0, 0)))


def _lane_row(vals, offset):
    return jnp.zeros((1, LANES), F32).at[0, offset:offset + vals.shape[0]].set(vals)


def _layer_params(l, w_in, conv_qkv, a_log, dt_bias, dn_norm_w, w_branch_a, sg_ln_g, sg_ln_b, w_spatial,
                  b_spatial, w_branch_b, w_out, ln1_g, ln1_b, w_up, conv_ffn, w_down, ln2_g, ln2_b):
    wi = w_in[l]
    return dict(
        wq=wi[:, :QKVZ_W].astype(BF16),
        wbd=jnp.pad(wi[:, BD_OFF:UV_OFF], ((0, 0), (0, LANES - 2 * DN_HEADS))).astype(BF16),
        wuv=wi[:, UV_OFF:GATE_OFF].astype(BF16),
        wg=wi[:, GATE_OFF:].astype(BF16),
        cw=_pad_rows(conv_qkv[l], SUBLANES),
        alog=_lane_row(a_log[l], DN_HEADS),
        dtb=_lane_row(dt_bias[l], DN_HEADS),
        nw=dn_norm_w[l].reshape(1, DN_DV),
        lng=sg_ln_g[l].reshape(1, SG_W),
        lnb=sg_ln_b[l].reshape(1, SG_W),
        ws=w_spatial[l],
        bst=b_spatial[l].T,
        wa=w_branch_a[l].astype(BF16),
        wb=w_branch_b[l].astype(BF16),
        wo=w_out[l].astype(BF16),
        ln1g=ln1_g[l].reshape(1, D_MODEL),
        ln1b=ln1_b[l].reshape(1, D_MODEL),
        wup=w_up[l].astype(BF16),
        cwf=_pad_rows(conv_ffn[l], SUBLANES),
        wdn=w_down[l].astype(BF16),
        ln2g=ln2_g[l].reshape(1, D_MODEL),
        ln2b=ln2_b[l].reshape(1, D_MODEL),
    )


def kernel(x, w_in, conv_qkv, a_log, dt_bias, dn_norm_w, w_branch_a, sg_ln_g, sg_ln_b, w_spatial, b_spatial, w_branch_b, w_out, ln1_g, ln1_b, w_up, conv_ffn, w_down, ln2_g, ln2_b):
    B, S, D = x.shape
    assert D == D_MODEL and S % max(TOKEN_TILE, IN_TILE) == 0 and B % DELTA_GROUP == 0
    xf = x.reshape(B * S, D)
    for l in range(w_in.shape[0]):
        p = _layer_params(l, w_in, conv_qkv, a_log, dt_bias, dn_norm_w, w_branch_a, sg_ln_g, sg_ln_b,
                          w_spatial, b_spatial, w_branch_b, w_out, ln1_g, ln1_b, w_up, conv_ffn, w_down,
                          ln2_g, ln2_b)
        xf = _layer(xf, B, S, p)
    return xf.reshape(B, S, D)
```

```python
import functools

import jax
import jax.numpy as jnp
from jax import lax
from jax.experimental import pallas as pl
from jax.experimental.pallas import tpu as pltpu

F32 = jnp.float32
BF16 = jnp.bfloat16

D_MODEL = 1024
DEPTH = 2
CHUNK = 64
DN_HEADS = 4
DN_DK = 128
DN_DV = 128
DN_CONV = 4
SG_GROUPS = 4
SG_GROUP_DIM = 128
SG_BLOCK = 128
FFN_DIM = 2816
FFN_CONV = 3
LN_EPS = 1e-5
RMS_EPS = 1e-6
L2_EPS = 1e-6
DEEPNORM_ALPHA = (2 * DEPTH) ** 0.25

QK_W = DN_HEADS * DN_DK
V_W = DN_HEADS * DN_DV
SG_W = SG_GROUPS * SG_GROUP_DIM
QKV_W = 2 * QK_W + V_W
QKVZ_W = QKV_W + V_W
BD_OFF = QKVZ_W
UV_OFF = BD_OFF + 2 * DN_HEADS
GATE_OFF = UV_OFF + 2 * SG_W

LANES = 128
SUBLANES = 8
PAIR = 2 * CHUNK
FFN_CHUNK = 256
N_FFN_CHUNKS = FFN_DIM // FFN_CHUNK
FFN_CONV_SLOTS = 4
TOKEN_TILE = 1024
FFN_TILE = 512
IN_TILE = 1024
ACT_BLOCK = 512
FFN_TAIL_CHUNKS = 4
MIX_TAIL_CHUNKS = 4
DELTA_TILE = 512
DELTA_GROUP = 1
SUB_TILE = 512
VMEM_LIMIT = 56 * 1024 * 1024


def _layer_norm(r, g, b):
    mu = jnp.mean(r, axis=-1, keepdims=True)
    d = r - mu
    var = jnp.mean(d * d, axis=-1, keepdims=True)
    return d * lax.rsqrt(var + LN_EPS) * g + b


def _dot(a, b):
    return jnp.dot(a, b, preferred_element_type=F32)


def _dot_nt(a, b):
    return lax.dot_general(a, b, (((1,), (1,)), ((), ())), preferred_element_type=F32)


def _dot_tn(a, b):
    return lax.dot_general(a, b, (((0,), (0,)), ((), ())), preferred_element_type=F32)


def _in_proj_kernel(x_ref, wq_ref, wbd_ref, wuv_ref, cw_ref, alog_ref, dtb_ref, lng_ref, lnb_ref,
                    ws_ref, bst_ref, act_ref, bg_ref, cbuf_ref, *, tm):
    halo = SUBLANES
    q_ref, k_ref, v_ref, sz_ref, ob_ref = [act_ref.at[:, i * ACT_BLOCK:(i + 1) * ACT_BLOCK] for i in range(5)]

    @pl.when(pl.program_id(1) == 0)
    def _():
        cbuf_ref[0:halo, :] = jnp.zeros((halo, QKV_W), F32)

    xb = x_ref[...].astype(BF16)
    proj = _dot(xb, wq_ref[...])
    pre = proj[:, :QKV_W]
    cbuf_ref[halo:halo + tm, :] = pre
    acc = pre * cw_ref[DN_CONV - 1:DN_CONV, :]
    for j in range(DN_CONV - 1):
        sh = DN_CONV - 1 - j
        acc = acc + cbuf_ref[halo - sh:halo - sh + tm, :] * cw_ref[j:j + 1, :]
    cbuf_ref[0:halo, :] = cbuf_ref[tm:tm + halo, :]
    qkv = acc * jax.nn.sigmoid(acc)
    for h in range(DN_HEADS):
        sl = slice(h * DN_DK, (h + 1) * DN_DK)
        qh = qkv[:, sl]
        qn = qh * lax.rsqrt(jnp.sum(qh * qh, axis=-1, keepdims=True) + L2_EPS) * (DN_DK ** -0.5)
        q_ref[:, sl] = qn.astype(BF16)
        kh = qkv[:, QK_W + h * DN_DK:QK_W + (h + 1) * DN_DK]
        kn = kh * lax.rsqrt(jnp.sum(kh * kh, axis=-1, keepdims=True) + L2_EPS)
        k_ref[:, sl] = kn.astype(BF16)
    v_ref[...] = qkv[:, 2 * QK_W:].astype(BF16)
    z = proj[:, QKV_W:]
    sz_ref[...] = (z * jax.nn.sigmoid(z)).astype(BF16)

    bd = _dot(xb, wbd_ref[...])
    beta = jax.nn.sigmoid(bd)
    a = bd + dtb_ref[...]
    g = -jnp.exp(alog_ref[...]) * (jnp.maximum(a, 0.0) + jnp.log1p(jnp.exp(-jnp.abs(a))))
    lane = lax.broadcasted_iota(jnp.int32, bd.shape, 1)
    bg_ref[...] = jnp.where(lane < DN_HEADS, beta, g)

    uv = _dot(xb, wuv_ref[...])
    ge = uv * (0.5 * (1.0 + jnp.tanh(0.7978845608028654 * (uv + 0.044715 * (uv * uv * uv)))))
    u = ge[:, :SG_W]
    vln = _layer_norm(ge[:, SG_W:], lng_ref[...], lnb_ref[...]).astype(BF16)
    ri = lax.broadcasted_iota(jnp.int32, (SG_BLOCK, SG_BLOCK), 0)
    ci = lax.broadcasted_iota(jnp.int32, (SG_BLOCK, SG_BLOCK), 1)
    nb = tm // SG_BLOCK
    for gi in range(SG_GROUPS):
        cs = slice(gi * SG_GROUP_DIM, (gi + 1) * SG_GROUP_DIM)
        wm = jnp.where(ci <= ri, ws_ref[gi], 0.0).astype(BF16)
        rhs = jnp.concatenate([vln[r * SG_BLOCK:(r + 1) * SG_BLOCK, cs] for r in range(nb)], axis=1)
        m = _dot(wm, rhs) + bst_ref[:, gi:gi + 1]
        for r in range(nb):
            rs = slice(r * SG_BLOCK, (r + 1) * SG_BLOCK)
            ob_ref[rs, cs] = (u[rs, cs] * m[:, r * SG_GROUP_DIM:(r + 1) * SG_GROUP_DIM]).astype(BF16)


def _bmm(a, b):
    return lax.dot_general(a, b, (((2,), (1,)), ((0,), (0,))), preferred_element_type=F32)


def _bmm_nt(a, b):
    return lax.dot_general(a, b, (((2,), (2,)), ((0,), (0,))), preferred_element_type=F32)


def _delta_prep(q_refs, k_refs, v_refs, bg_refs, w_refs, qg_refs, u_refs, a_refs, kdt_refs, gl_refs, tt):
    P, H = PAIR, DN_HEADS
    npair = tt // P
    ri = lax.broadcasted_iota(jnp.int32, (P, P), 0)
    ci = lax.broadcasted_iota(jnp.int32, (P, P), 1)
    same = (ri // CHUNK) == (ci // CHUNK)
    eye = ri == ci
    causal = same & (ci <= ri)
    strict = same & (ci < ri)
    lastm = ci == (ri // CHUNK) * CHUNK + (CHUNK - 1)
    lvl = []
    s = 1
    while s < CHUNK:
        lvl.append(((ri // (2 * s)) == (ci // (2 * s))) & (((ri // s) % 2) == 1) & (((ci // s) % 2) == 0))
        s *= 2

    nseq = len(q_refs)

    def tiles(refs):
        return jnp.stack([ref[p * P:(p + 1) * P, h * DN_DK:(h + 1) * DN_DK]
                          for ref in refs for p in range(npair) for h in range(H)])

    def untile(x, i):
        base = i * npair * H
        return jnp.concatenate([jnp.concatenate([x[base + p * H + h] for h in range(H)], axis=1)
                                for p in range(npair)], axis=0)

    q16 = tiles(q_refs)
    k16 = tiles(k_refs)
    v16 = tiles(v_refs)
    bgs = [ref[...] for ref in bg_refs]
    bcol = jnp.stack([bg[p * P:(p + 1) * P, h:h + 1] for bg in bgs for p in range(npair) for h in range(H)])
    gcol = jnp.stack([bg[p * P:(p + 1) * P, H + h:H + h + 1]
                      for bg in bgs for p in range(npair) for h in range(H)])
    grow = jnp.sum(jnp.where(eye, gcol, 0.0), axis=1, keepdims=True)
    Gcol = jnp.sum(jnp.where(causal, grow, 0.0), axis=2, keepdims=True)
    Grow = jnp.sum(jnp.where(eye, Gcol, 0.0), axis=1, keepdims=True)
    Glast = jnp.sum(jnp.where(lastm, Grow, 0.0), axis=2, keepdims=True)
    D = jnp.where(causal, jnp.exp(jnp.where(causal, Gcol - Grow, 0.0)), 0.0)
    bcol16 = bcol.astype(BF16)
    eG16 = jnp.exp(Gcol).astype(BF16)
    kb = k16 * bcol16
    kbg = kb * eG16
    qg = q16 * eG16
    kd = k16 * jnp.exp(Glast - Gcol).astype(BF16)
    vb = v16 * bcol16
    L = jnp.where(strict, _bmm_nt(kb, k16) * D, 0.0).astype(BF16)
    A = jnp.where(causal, _bmm_nt(q16, k16) * D, 0.0)
    def bf16_mask(m):
        return jnp.where(m, 1.0, 0.0).astype(BF16)

    T = bf16_mask(eye) - L * bf16_mask(lvl[0])
    for m in lvl[1:]:
        X = _bmm(L * bf16_mask(m), T).astype(BF16)
        T = T - _bmm(T, X).astype(BF16)
    WU = _bmm(T, jnp.concatenate([kbg, vb], axis=2))
    kdt = jnp.swapaxes(kd.astype(F32), 1, 2)
    glf = jnp.broadcast_to(jnp.exp(Glast), (nseq * npair * H, P, DN_DV))
    gl8 = jnp.concatenate([glf[:, c * CHUNK:c * CHUNK + SUBLANES, :] for c in range(P // CHUNK)], axis=1)
    for i in range(nseq):
        w_refs[i][...] = untile(WU[:, :, :DN_DK], i).astype(BF16)
        u_refs[i][...] = untile(WU[:, :, DN_DK:], i)
        qg_refs[i][...] = untile(qg, i)
        a_refs[i][...] = untile(A, i).astype(BF16)
        kdt_refs[i][...] = untile(kdt, i).astype(BF16)
        gl_refs[i][...] = untile(gl8, i)


def _delta_kernel(q_ref, k_ref, v_ref, bg_ref, sz_ref, nw_ref, o_ref,
                  s_ref, w_ref, qg_ref, u_ref, a_ref, kdt_ref, gl_ref, *, tc, nbat, group):
    P, H = PAIR, DN_HEADS

    @pl.when(pl.program_id(0) == 0)
    def _():
        s_ref[...] = jnp.zeros(s_ref.shape, F32)

    for g0 in range(0, nbat, group):
        bs = range(g0, g0 + group)
        _delta_prep([q_ref.at[b] for b in bs], [k_ref.at[b] for b in bs], [v_ref.at[b] for b in bs],
                    [bg_ref.at[b] for b in bs], [w_ref.at[b] for b in bs], [qg_ref.at[b] for b in bs],
                    [u_ref.at[b] for b in bs], [a_ref.at[b] for b in bs], [kdt_ref.at[b] for b in bs],
                    [gl_ref.at[b] for b in bs], tc)

    nw = nw_ref[...]

    def chains(ref, rows):
        return jnp.stack([ref[b, rows, h * DN_DV:(h + 1) * DN_DV] for b in range(nbat) for h in range(H)])

    def pair(p, carry):
        r0 = pl.multiple_of(p * P, P)
        kdt = chains(kdt_ref, pl.ds(r0, P))
        S = s_ref[...]
        for c in range(P // CHUNK):
            rows = pl.ds(pl.multiple_of(r0 + c * CHUNK, CHUNK), CHUNK)
            lhs1 = jnp.concatenate([chains(w_ref, rows), chains(qg_ref, rows)], axis=1)
            WS = _bmm(lhs1, S.astype(BF16))
            ub = (chains(u_ref, rows) - WS[:, :CHUNK]).astype(BF16)
            z = jnp.zeros_like(ub)
            rhs2 = jnp.concatenate([ub, z] if c == 0 else [z, ub], axis=1)
            R = _bmm(jnp.concatenate([chains(a_ref, rows), kdt], axis=1), rhs2)
            o = WS[:, CHUNK:] + R[:, :CHUNK]
            grow8 = pl.ds(pl.multiple_of(p * (P // SUBLANES) + c * SUBLANES, SUBLANES), SUBLANES)
            S = S * chains(gl_ref, grow8)[:, 0:1, :] + R[:, CHUNK:]
            o = o * lax.rsqrt(jnp.mean(o * o, axis=-1, keepdims=True) + RMS_EPS) * nw
            o = (o * chains(sz_ref, rows).astype(F32)).astype(BF16)
            o_ref[:, rows, :] = jnp.stack([jnp.concatenate([o[b * H + h] for h in range(H)], axis=1)
                                           for b in range(nbat)])
        s_ref[...] = S
        return carry

    lax.fori_loop(0, tc // P, pair, 0)


def _mix_out_kernel(x_ref, oa_ref, ob_ref, wg_ref, wa_ref, wb_ref, wo_ref, g_ref, b_ref, out_ref):
    x = x_ref[...]
    gates = jax.nn.sigmoid(_dot(x.astype(BF16), wg_ref[...]))
    ya = _dot(oa_ref[...], wa_ref[...])
    yb = _dot(ob_ref[...], wb_ref[...])
    h = gates[:, :D_MODEL] * ya + gates[:, D_MODEL:] * yb
    hb = h.astype(BF16)
    rt = x.shape[0] // MIX_TAIL_CHUNKS
    for c in range(MIX_TAIL_CHUNKS):
        rr = slice(c * rt, (c + 1) * rt)
        y = _dot(hb[rr], wo_ref[...])
        out_ref[rr, :] = _layer_norm(DEEPNORM_ALPHA * x[rr] + y, g_ref[...], b_ref[...])


def _ffn_kernel(x_ref, wup_ref, cw_ref, wdn_ref, g_ref, b_ref, out_ref, carry_ref, buf_ref, hm_ref, xb_ref, *, tm, sub):
    halo = SUBLANES

    @pl.when(pl.program_id(1) == 0)
    def _():
        carry_ref[...] = jnp.zeros(carry_ref.shape, F32)

    nslot = buf_ref.shape[0]

    def cols(j):
        return [slice(h * FFN_DIM + j * FFN_CHUNK, h * FFN_DIM + (j + 1) * FFN_CHUNK) for h in range(2)]

    for r in range(tm // sub):
        rows = slice(r * sub, (r + 1) * sub)
        xb_ref[rows, :] = x_ref[rows, :].astype(BF16)

        def up(j):
            return [_dot(xb_ref[rows, :], wup_ref[:, cs]) for cs in cols(j)]

        def conv_half(i, p, cs):
            slot = buf_ref.at[i % nslot]
            slot[0:halo, :] = carry_ref[:, cs]
            slot[halo:halo + sub, :] = p
            c = p * cw_ref[FFN_CONV - 1:FFN_CONV, cs]
            for t in range(FFN_CONV - 1):
                sh = FFN_CONV - 1 - t
                c = c + slot[halo - sh:halo - sh + sub, :] * cw_ref[t:t + 1, cs]
            carry_ref[:, cs] = p[sub - halo:, :]
            return c

        ps = up(0)
        for j in range(N_FFN_CHUNKS):
            ps_next = up(j + 1) if j + 1 < N_FFN_CHUNKS else None
            a, b = [conv_half(2 * (r * N_FFN_CHUNKS + j) + h, ps[h], cs) for h, cs in enumerate(cols(j))]
            ps = ps_next
            hm_ref[rows, j * FFN_CHUNK:(j + 1) * FFN_CHUNK] = (a * jax.nn.sigmoid(a) * b).astype(BF16)
        rt = sub // FFN_TAIL_CHUNKS
        for c in range(FFN_TAIL_CHUNKS):
            rr = slice(r * sub + c * rt, r * sub + (c + 1) * rt)
            y = _dot(hm_ref[rr, :], wdn_ref[...])
            out_ref[rr, :] = _layer_norm(DEEPNORM_ALPHA * x_ref[rr, :] + y, g_ref[...], b_ref[...])


def _const_spec(shape):
    nd = len(shape)
    return pl.BlockSpec(shape, lambda *_: (0,) * nd)


def _params(sem):
    return pltpu.CompilerParams(dimension_semantics=sem, vmem_limit_bytes=VMEM_LIMIT)


def _layer(xf, B, S, p):
    T = B * S
    tm = TOKEN_TILE
    ns = S // tm
    seq = ("arbitrary", "arbitrary")

    def tok(width):
        return pl.BlockSpec((tm, width), lambda b, s: (b * ns + s, 0))

    ti = IN_TILE
    ni = S // ti

    def toki(width):
        return pl.BlockSpec((ti, width), lambda b, s: (b * ni + s, 0))

    act, bg = pl.pallas_call(
        functools.partial(_in_proj_kernel, tm=ti),
        grid=(B, ni),
        in_specs=[toki(D_MODEL), _const_spec(p["wq"].shape), _const_spec(p["wbd"].shape),
                  _const_spec(p["wuv"].shape), _const_spec(p["cw"].shape), _const_spec(p["alog"].shape),
                  _const_spec(p["dtb"].shape), _const_spec(p["lng"].shape), _const_spec(p["lnb"].shape),
                  _const_spec(p["ws"].shape), _const_spec(p["bst"].shape)],
        out_specs=[toki(5 * ACT_BLOCK), toki(LANES)],
        out_shape=[jax.ShapeDtypeStruct((T, 5 * ACT_BLOCK), BF16), jax.ShapeDtypeStruct((T, LANES), F32)],
        scratch_shapes=[pltpu.VMEM((ti + SUBLANES, QKV_W), F32)],
        compiler_params=_params(seq),
        name="in_proj",
    )(xf, p["wq"], p["wbd"], p["wuv"], p["cw"], p["alog"], p["dtb"], p["lng"], p["lnb"], p["ws"], p["bst"])

    tc = DELTA_TILE

    def seq3(width):
        return pl.BlockSpec((B, tc, width), lambda s: (0, s, 0))

    def actcol(j):
        return pl.BlockSpec((B, tc, ACT_BLOCK), lambda s: (0, s, j))

    def b3(t):
        return t.reshape(B, S, t.shape[-1])

    def scr(rows, dtype):
        return pltpu.VMEM((B, rows, V_W), dtype)

    oa = pl.pallas_call(
        functools.partial(_delta_kernel, tc=tc, nbat=B, group=DELTA_GROUP),
        grid=(S // tc,),
        in_specs=[actcol(0), actcol(1), actcol(2), seq3(LANES), actcol(3), _const_spec(p["nw"].shape)],
        out_specs=seq3(V_W),
        out_shape=jax.ShapeDtypeStruct((B, S, V_W), BF16),
        scratch_shapes=[pltpu.VMEM((B * DN_HEADS, DN_DK, DN_DV), F32),
                        scr(tc, BF16), scr(tc, BF16), scr(tc, F32), scr(tc, BF16), scr(tc, BF16),
                        scr(tc // SUBLANES, F32)],
        compiler_params=_params(("arbitrary",)),
        name="delta",
    )(b3(act), b3(act), b3(act), b3(bg), b3(act), p["nw"]).reshape(T, V_W)

    def tok1(width):
        return pl.BlockSpec((tm, width), lambda i: (i, 0))

    x1 = pl.pallas_call(
        _mix_out_kernel,
        grid=(T // tm,),
        in_specs=[tok1(D_MODEL), tok1(V_W), pl.BlockSpec((tm, ACT_BLOCK), lambda i: (i, 4)), _const_spec(p["wg"].shape), _const_spec(p["wa"].shape),
                  _const_spec(p["wb"].shape), _const_spec(p["wo"].shape), _const_spec(p["ln1g"].shape),
                  _const_spec(p["ln1b"].shape)],
        out_specs=tok1(D_MODEL),
        out_shape=jax.ShapeDtypeStruct((T, D_MODEL), F32),
        compiler_params=_params(("arbitrary",)),
        name="mix_out",
    )(xf, oa, act, p["wg"], p["wa"], p["wb"], p["wo"], p["ln1g"], p["ln1b"])

    tf = FFN_TILE
    nf = S // tf
    tokf = pl.BlockSpec((tf, D_MODEL), lambda b, s: (b * nf + s, 0))
    x2 = pl.pallas_call(
        functools.partial(_ffn_kernel, tm=tf, sub=SUB_TILE),
        grid=(B, nf),
        in_specs=[tokf, _const_spec(p["wup"].shape), _const_spec(p["cwf"].shape),
                  _const_spec(p["wdn"].shape), _const_spec(p["ln2g"].shape), _const_spec(p["ln2b"].shape)],
        out_specs=tokf,
        out_shape=jax.ShapeDtypeStruct((T, D_MODEL), F32),
        scratch_shapes=[pltpu.VMEM((SUBLANES, 2 * FFN_DIM), F32),
                        pltpu.VMEM((FFN_CONV_SLOTS, SUB_TILE + SUBLANES, FFN_CHUNK), F32),
                        pltpu.VMEM((tf, FFN_DIM), BF16),
                        pltpu.VMEM((tf, D_MODEL), BF16)],
        compiler_params=_params(seq),
        name="ffn",
    )(x1, p["wup"], p["cwf"], p["wdn"], p["ln2g"], p["ln2b"])
    return x2


def _pad_rows(a, rows):
    return jnp.pad(a, ((0, rows - a.shape[0]), (0, 0)))


def _lane_row(vals, offset):
    return jnp.zeros((1, LANES), F32).at[0, offset:offset + vals.shape[0]].set(vals)


def _layer_params(l, w_in, conv_qkv, a_log, dt_bias, dn_norm_w, w_branch_a, sg_ln_g, sg_ln_b, w_spatial,
                  b_spatial, w_branch_b, w_out, ln1_g, ln1_b, w_up, conv_ffn, w_down, ln2_g, ln2_b):
    wi = w_in[l]
    return dict(
        wq=wi[:, :QKVZ_W].astype(BF16),
        wbd=jnp.pad(wi[:, BD_OFF:UV_OFF], ((0, 0), (0, LANES - 2 * DN_HEADS))).astype(BF16),
        wuv=wi[:, UV_OFF:GATE_OFF].astype(BF16),
        wg=wi[:, GATE_OFF:].astype(BF16),
        cw=_pad_rows(conv_qkv[l], SUBLANES),
        alog=_lane_row(a_log[l], DN_HEADS),
        dtb=_lane_row(dt_bias[l], DN_HEADS),
        nw=dn_norm_w[l].reshape(1, DN_DV),
        lng=sg_ln_g[l].reshape(1, SG_W),
        lnb=sg_ln_b[l].reshape(1, SG_W),
        ws=w_spatial[l],
        bst=b_spatial[l].T,
        wa=w_branch_a[l].astype(BF16),
        wb=w_branch_b[l].astype(BF16),
        wo=w_out[l].astype(BF16),
        ln1g=ln1_g[l].reshape(1, D_MODEL),
        ln1b=ln1_b[l].reshape(1, D_MODEL),
        wup=w_up[l].astype(BF16),
        cwf=_pad_rows(conv_ffn[l], SUBLANES),
        wdn=w_down[l].astype(BF16),
        ln2g=ln2_g[l].reshape(1, D_MODEL),
        ln2b=ln2_b[l].reshape(1, D_MODEL),
    )


def kernel(x, w_in, conv_qkv, a_log, dt_bias, dn_norm_w, w_branch_a, sg_ln_g, sg_ln_b, w_spatial, b_spatial, w_branch_b, w_out, ln1_g, ln1_b, w_up, conv_ffn, w_down, ln2_g, ln2_b):
    B, S, D = x.shape
    assert D == D_MODEL and S % max(TOKEN_TILE, IN_TILE) == 0 and B % DELTA_GROUP == 0
    xf = x.reshape(B * S, D)
    for l in range(w_in.shape[0]):
        p = _layer_params(l, w_in, conv_qkv, a_log, dt_bias, dn_norm_w, w_branch_a, sg_ln_g, sg_ln_b,
                          w_spatial, b_spatial, w_branch_b, w_out, ln1_g, ln1_b, w_up, conv_ffn, w_down,
                          ln2_g, ln2_b)
        xf = _layer(xf, B, S, p)
    return xf.reshape(B, S, D)
```
